```python
import jax
import jax.numpy as jnp
from jax import lax
import numpy as np

D_MODEL = 4096
BATCH = 2
SEQ = 4096
DEPTH = 1
DEC_BATCH = 32
DEC_SEQ = 64
PAST_LEN = 4096

CHUNK = 64
MIX_WIDTH = D_MODEL
ATTN_WIDTH = MIX_WIDTH // 2
RET_WIDTH = MIX_WIDTH - ATTN_WIDTH
HEAD_DIM = 64
N_HEADS = ATTN_WIDTH // HEAD_DIM
N_KV_HEADS = 4
GROUP = N_HEADS // N_KV_HEADS
WINDOW = 128
WINDOW_CHUNKS = WINDOW // CHUNK
RET_HEADS = 8
RET_DV = RET_WIDTH // RET_HEADS
RET_DK = RET_DV // 2
ROPE_BASE = 10000.0
D_FF = 256 * ((8 * D_MODEL // 3 + 255) // 256)
PLE_DIM = 256
EPS = 1e-6

ATTN_Q_COLS = N_HEADS * HEAD_DIM
ATTN_KV_COLS = N_KV_HEADS * HEAD_DIM
RET_QK_COLS = RET_HEADS * RET_DK
SPLIT_POINTS = (
    ATTN_Q_COLS,
    ATTN_Q_COLS + ATTN_KV_COLS,
    ATTN_Q_COLS + 2 * ATTN_KV_COLS,
    ATTN_Q_COLS + 2 * ATTN_KV_COLS + RET_QK_COLS,
    ATTN_Q_COLS + 2 * ATTN_KV_COLS + 2 * RET_QK_COLS,
    ATTN_Q_COLS + 2 * ATTN_KV_COLS + 2 * RET_QK_COLS + RET_WIDTH,
)
IN_WIDTH = ATTN_Q_COLS + 2 * ATTN_KV_COLS + 2 * RET_QK_COLS + 2 * RET_WIDTH

kernel_name = "hymba_swa_sink_retention_macaron_ple_stream_step"


def rmsnorm(x, g):
    xf = x.astype(jnp.float32)
    y = xf * lax.rsqrt(jnp.mean(xf * xf, axis=-1, keepdims=True) + EPS)
    return (y * g.astype(jnp.float32)).astype(x.dtype)


def swiglu_half_step(h, g, w_gate, w_up, w_down):
    z = rmsnorm(h, g)
    return h + 0.5 * ((jax.nn.silu(z @ w_gate) * (z @ w_up)) @ w_down)


def ple_add(h, p, g, w_gate, w_proj):
    gate = jax.nn.sigmoid(rmsnorm(h, g) @ w_gate)
    return h + gate * (p @ w_proj)


def rotary(x, pos):
    half = x.shape[-1] // 2
    inv_freq = ROPE_BASE ** (-jnp.arange(half, dtype=jnp.float32) / half)
    ang = pos[:, None] * inv_freq[None, :]
    cos = jnp.cos(ang)[:, None, :]
    sin = jnp.sin(ang)[:, None, :]
    x1, x2 = x[..., :half], x[..., half:]
    return jnp.concatenate([x1 * cos - x2 * sin, x2 * cos + x1 * sin], axis=-1)


def retention_log_decay():
    return jnp.log(1.0 - 2.0 ** (-5.0 - jnp.arange(RET_HEADS, dtype=jnp.float32)))


def mixer_project(z, pos, w_in, g_q, g_k):
    B, L = z.shape[0], z.shape[1]
    qa, ka, va, qr, kr, vr, gr = jnp.split(z @ w_in, SPLIT_POINTS, axis=-1)
    qa = rmsnorm(qa.reshape(B, L, N_HEADS, HEAD_DIM), g_q)
    ka = rmsnorm(ka.reshape(B, L, N_KV_HEADS, HEAD_DIM), g_k)
    va = va.reshape(B, L, N_KV_HEADS, HEAD_DIM)
    qr = rotary(qr.reshape(B, L, RET_HEADS, RET_DK).astype(jnp.float32), pos)
    kr = rotary(kr.reshape(B, L, RET_HEADS, RET_DK).astype(jnp.float32), pos) * (RET_DK ** -0.5)
    vr = vr.reshape(B, L, RET_HEADS, RET_DV).astype(jnp.float32)
    return qa, ka, va, qr, kr, vr, gr


def sink_attend(q, k, v, sinks, mask):
    scores = jnp.einsum("...qkgd,...tkd->...kgqt", q, k).astype(jnp.float32) * (HEAD_DIM ** -0.5)
    if mask is not None:
        scores = jnp.where(mask, scores, -jnp.inf)
    sink = sinks.astype(jnp.float32).reshape(N_KV_HEADS, GROUP, 1, 1)
    m = jnp.maximum(scores.max(axis=-1, keepdims=True), sink)
    e = jnp.exp(scores - m)
    probs = e / (e.sum(axis=-1, keepdims=True) + jnp.exp(sink - m))
    return jnp.einsum("...kgqt,...tkd->...qkgd", probs.astype(v.dtype), v)


def attn_prompt(q, k, v, sinks):
    B, S = q.shape[0], q.shape[1]
    n_chunks = S // CHUNK
    qb = q.reshape(B, n_chunks, CHUNK, N_KV_HEADS, GROUP, HEAD_DIM)
    pad = ((0, 0), (WINDOW, 0), (0, 0), (0, 0))
    kp = jnp.pad(k, pad).reshape(B, n_chunks + WINDOW_CHUNKS, CHUNK, N_KV_HEADS, HEAD_DIM)
    vp = jnp.pad(v, pad).reshape(B, n_chunks + WINDOW_CHUNKS, CHUNK, N_KV_HEADS, HEAD_DIM)
    kb = jnp.concatenate([kp[:, j:j + n_chunks] for j in range(WINDOW_CHUNKS + 1)], axis=2)
    vb = jnp.concatenate([vp[:, j:j + n_chunks] for j in range(WINDOW_CHUNKS + 1)], axis=2)
    c = jnp.arange(n_chunks)[:, None]
    j = jnp.arange((WINDOW_CHUNKS + 1) * CHUNK)[None, :]
    valid = (c + j // CHUNK) >= WINDOW_CHUNKS
    mask = valid[:, None, None, None, :]
    o = sink_attend(qb, kb, vb, sinks, mask)
    return o.reshape(B, S, ATTN_WIDTH)


def retention_block(state, q, k, v, log_gamma):
    L = q.shape[1]
    idx = jnp.arange(L, dtype=jnp.float32)
    decay = jnp.exp(log_gamma[:, None, None] * jnp.abs(idx[:, None] - idx[None, :]))
    scores = jnp.einsum("blhd,bmhd->bhlm", q, k) * decay[None]
    out = jnp.einsum("bhlm,bmhv->blhv", scores, v)
    q_decay = jnp.exp(log_gamma[None, :] * (idx[:, None] + 1.0))
    out = out + jnp.einsum("blhd,bhdv->blhv", q, state) * q_decay[None, :, :, None]
    k_decay = jnp.exp(log_gamma[None, :] * (L - 1.0 - idx)[:, None])
    new_state = (jnp.exp(log_gamma * L)[None, :, None, None] * state
                 + jnp.einsum("blhd,blhv->bhdv", k * k_decay[None, :, :, None], v))
    return out, new_state


def retention_prompt(q, k, v, log_gamma):
    B, S = q.shape[0], q.shape[1]
    n_chunks = S // CHUNK

    def to_chunks(a):
        return jnp.swapaxes(a.reshape((B, n_chunks, CHUNK) + a.shape[2:]), 0, 1)

    state0 = jnp.zeros((B, RET_HEADS, RET_DK, RET_DV), jnp.float32)

    def step(state, xs):
        qc, kc, vc = xs
        out, state = retention_block(state, qc, kc, vc, log_gamma)
        return state, out

    state, out = lax.scan(step, state0, (to_chunks(q), to_chunks(k), to_chunks(v)))
    return jnp.swapaxes(out, 0, 1).reshape(B, S, RET_HEADS, RET_DV), state


def mixer_merge(attn_out, ret_out, gate, g_ret, w_out):
    B, L = gate.shape[0], gate.shape[1]
    mu = jnp.mean(ret_out, axis=-1, keepdims=True)
    var = jnp.mean(jnp.square(ret_out - mu), axis=-1, keepdims=True)
    normed = (ret_out - mu) * lax.rsqrt(var + EPS) * g_ret.astype(jnp.float32)
    ret = normed.reshape(B, L, RET_WIDTH).astype(gate.dtype) * jax.nn.silu(gate)
    return jnp.concatenate([attn_out, ret], axis=-1) @ w_out


def setup_inputs(seed: int = 0) -> dict:
    key = jax.random.key(seed)
    ks = jax.random.split(key, 32)
    f32 = jnp.float32

    def nrm(k, shape, scale=1.0):
        return jax.random.normal(k, shape, f32) * scale

    def gain(k, shape):
        return 1.0 + 0.05 * jax.random.normal(k, shape, f32)

    return {
        "x_prompt": nrm(ks[0], (BATCH, SEQ, D_MODEL)),
        "x_sample": nrm(ks[1], (DEC_BATCH, DEC_SEQ, D_MODEL)),
        "cache_attn_k": nrm(ks[2], (DEPTH, DEC_BATCH, WINDOW, N_KV_HEADS, HEAD_DIM)),
        "cache_attn_v": nrm(ks[3], (DEPTH, DEC_BATCH, WINDOW, N_KV_HEADS, HEAD_DIM)),
        "state_ret": nrm(ks[4], (DEPTH, DEC_BATCH, RET_HEADS, RET_DK, RET_DV)),
        "p_prompt": nrm(ks[5], (DEPTH, BATCH, SEQ, PLE_DIM)),
        "p_sample": nrm(ks[6], (DEPTH, DEC_BATCH, DEC_SEQ, PLE_DIM)),
        "g_ffn1": gain(ks[7], (DEPTH, D_MODEL)),
        "w_ffn1_gate": nrm(ks[8], (DEPTH, D_MODEL, D_FF), D_MODEL ** -0.5),
        "w_ffn1_up": nrm(ks[9], (DEPTH, D_MODEL, D_FF), D_MODEL ** -0.5),
        "w_ffn1_down": nrm(ks[10], (DEPTH, D_FF, D_MODEL), D_FF ** -0.5),
        "g_mix": gain(ks[11], (DEPTH, D_MODEL)),
        "w_in": nrm(ks[12], (DEPTH, D_MODEL, IN_WIDTH), D_MODEL ** -0.5),
        "g_q": gain(ks[13], (DEPTH, HEAD_DIM)),
        "g_k": gain(ks[14], (DEPTH, HEAD_DIM)),
        "attn_sinks": nrm(ks[15], (DEPTH, N_HEADS)),
        "g_ret": gain(ks[16], (DEPTH, RET_HEADS, RET_DV)),
        "w_out": nrm(ks[17], (DEPTH, MIX_WIDTH, D_MODEL), MIX_WIDTH ** -0.5),
        "g_ffn2": gain(ks[18], (DEPTH, D_MODEL)),
        "w_ffn2_gate": nrm(ks[19], (DEPTH, D_MODEL, D_FF), D_MODEL ** -0.5),
        "w_ffn2_up": nrm(ks[20], (DEPTH, D_MODEL, D_FF), D_MODEL ** -0.5),
        "w_ffn2_down": nrm(ks[21], (DEPTH, D_FF, D_MODEL), D_FF ** -0.5),
        "g_ple": gain(ks[22], (DEPTH, D_MODEL)),
        "w_ple_gate": nrm(ks[23], (DEPTH, D_MODEL, D_MODEL), D_MODEL ** -0.5),
        "w_ple_proj": nrm(ks[24], (DEPTH, PLE_DIM, D_MODEL), PLE_DIM ** -0.5),
    }


def reference(x_prompt, x_sample, cache_attn_k, cache_attn_v, state_ret, p_prompt, p_sample,
              g_ffn1, w_ffn1_gate, w_ffn1_up, w_ffn1_down, g_mix, w_in, g_q, g_k, attn_sinks,
              g_ret, w_out, g_ffn2, w_ffn2_gate, w_ffn2_up, w_ffn2_down, g_ple, w_ple_gate,
              w_ple_proj):
    log_gamma = retention_log_decay()
    dec_b, dec_len = x_sample.shape[0], x_sample.shape[1]
    pos_p = jnp.arange(x_prompt.shape[1], dtype=jnp.float32)
    pos_s = PAST_LEN + jnp.arange(dec_len, dtype=jnp.float32)

    hp, hs = x_prompt, x_sample
    kp_list, vp_list, sp_list, ks_list, vs_list, ss_list = [], [], [], [], [], []
    for i in range(DEPTH):
        hp = swiglu_half_step(hp, g_ffn1[i], w_ffn1_gate[i], w_ffn1_up[i], w_ffn1_down[i])
        hs = swiglu_half_step(hs, g_ffn1[i], w_ffn1_gate[i], w_ffn1_up[i], w_ffn1_down[i])

        qa, ka, va, qr, kr, vr, gr = mixer_project(rmsnorm(hp, g_mix[i]), pos_p, w_in[i], g_q[i], g_k[i])
        attn_o = attn_prompt(qa, ka, va, attn_sinks[i])
        ret_o, ret_state_p = retention_prompt(qr, kr, vr, log_gamma)
        hp = hp + mixer_merge(attn_o, ret_o, gr, g_ret[i], w_out[i])
        kp_list.append(ka[:, -WINDOW:])
        vp_list.append(va[:, -WINDOW:])
        sp_list.append(ret_state_p.astype(x_prompt.dtype))

        qa, ka, va, qr, kr, vr, gr = mixer_project(rmsnorm(hs, g_mix[i]), pos_s, w_in[i], g_q[i], g_k[i])
        k_all = jnp.concatenate([cache_attn_k[i], ka], axis=1)
        v_all = jnp.concatenate([cache_attn_v[i], va], axis=1)
        attn_o = sink_attend(qa.reshape(dec_b, dec_len, N_KV_HEADS, GROUP, HEAD_DIM), k_all, v_all,
                             attn_sinks[i], None).reshape(dec_b, dec_len, ATTN_WIDTH)
        ret_o, ret_state_s = retention_block(state_ret[i].astype(jnp.float32), qr, kr, vr, log_gamma)
        hs = hs + mixer_merge(attn_o, ret_o, gr, g_ret[i], w_out[i])
        ks_list.append(k_all[:, -WINDOW:])
        vs_list.append(v_all[:, -WINDOW:])
        ss_list.append(ret_state_s.astype(state_ret.dtype))

        hp = swiglu_half_step(hp, g_ffn2[i], w_ffn2_gate[i], w_ffn2_up[i], w_ffn2_down[i])
        hs = swiglu_half_step(hs, g_ffn2[i], w_ffn2_gate[i], w_ffn2_up[i], w_ffn2_down[i])

        hp = ple_add(hp, p_prompt[i], g_ple[i], w_ple_gate[i], w_ple_proj[i])
        hs = ple_add(hs, p_sample[i], g_ple[i], w_ple_gate[i], w_ple_proj[i])

    new_attn_k_prompt = jnp.stack(kp_list, axis=0)
    new_attn_v_prompt = jnp.stack(vp_list, axis=0)
    new_state_ret_prompt = jnp.stack(sp_list, axis=0)
    new_attn_k_sample = jnp.stack(ks_list, axis=0)
    new_attn_v_sample = jnp.stack(vs_list, axis=0)
    new_state_ret_sample = jnp.stack(ss_list, axis=0)
    return (hp, hs, new_attn_k_prompt, new_attn_v_prompt, new_state_ret_prompt,
            new_attn_k_sample, new_attn_v_sample, new_state_ret_sample)
```

```python
import functools
import math

import numpy as np
import jax
import jax.numpy as jnp
from jax import lax
from jax.experimental import pallas as pl
from jax.experimental.pallas import tpu as pltpu

F32 = jnp.float32
BF16 = jnp.bfloat16

D_MODEL = 4096
CHUNK = 64
HEAD_DIM = 64
N_HEADS = 32
N_KV_HEADS = 4
GROUP = N_HEADS // N_KV_HEADS
WINDOW = 128
RET_HEADS = 8
RET_DK = 128
RET_DV = 256
ATTN_WIDTH = N_HEADS * HEAD_DIM
RET_WIDTH = RET_HEADS * RET_DV
PAST_LEN = 4096
ROPE_BASE = 10000.0
EPS = 1e-6

Q_OFF = 0
K_OFF = Q_OFF + N_HEADS * HEAD_DIM
V_OFF = K_OFF + N_KV_HEADS * HEAD_DIM
QR_OFF = V_OFF + N_KV_HEADS * HEAD_DIM
KR_OFF = QR_OFF + RET_HEADS * RET_DK
VR_OFF = KR_OFF + RET_HEADS * RET_DK
GR_OFF = VR_OFF + RET_WIDTH
IN_WIDTH = GR_OFF + RET_WIDTH
KV_COLS = N_KV_HEADS * HEAD_DIM

LOG_GAMMA = tuple(math.log(1.0 - 2.0 ** (-5.0 - h)) for h in range(RET_HEADS))

V7X_VMEM_BYTES = 64 * 1024 * 1024
VMEM_CAP_BYTES = 60000 * 1024
LANE = 128

FFN_TM = 512
FFN_TF = 256
MM_TM = 512
MM_TN = 512
NORM_ROWS = 16


def _vmem_limit(estimate_bytes):
    return int(min(VMEM_CAP_BYTES, max(32 * 1024 * 1024, estimate_bytes * 5 // 4)))


def _params(semantics, vmem_estimate):
    return pltpu.CompilerParams(dimension_semantics=semantics,
                                vmem_limit_bytes=_vmem_limit(vmem_estimate))


def _rmsnorm_rows(x_ref, g_ref, z_ref):
    g = g_ref[...]

    def body(r, carry):
        rows = pl.ds(pl.multiple_of(r * NORM_ROWS, NORM_ROWS), NORM_ROWS)
        x = x_ref[rows, :]
        ms = jnp.mean(x * x, axis=-1, keepdims=True)
        z_ref[rows, :] = (x * lax.rsqrt(ms + EPS) * g).astype(z_ref.dtype)
        return carry

    lax.fori_loop(0, x_ref.shape[0] // NORM_ROWS, body, 0)


def _ffn_body(x_ref, g_ref, wg_ref, wu_ref, wd_ref, o_ref, z_ref, *, n_f):
    f = pl.program_id(1)
    n_row_steps = x_ref.shape[0] // NORM_ROWS

    @pl.when(f == 0)
    def _():
        _rmsnorm_rows(x_ref, g_ref, z_ref)

        def zero(r, carry):
            rows = pl.ds(pl.multiple_of(r * NORM_ROWS, NORM_ROWS), NORM_ROWS)
            o_ref[rows, :] = jnp.zeros((NORM_ROWS, o_ref.shape[1]), F32)
            return carry

        lax.fori_loop(0, n_row_steps, zero, 0)

    z = z_ref[...]
    gate = jnp.dot(z, wg_ref[...], preferred_element_type=F32)
    up = jnp.dot(z, wu_ref[...], preferred_element_type=F32)
    a = (gate * jax.nn.sigmoid(gate) * up).astype(BF16)
    o_ref[...] += jnp.dot(a, wd_ref[...], preferred_element_type=F32)

    @pl.when(f == n_f - 1)
    def _():
        def fin(r, carry):
            rows = pl.ds(pl.multiple_of(r * NORM_ROWS, NORM_ROWS), NORM_ROWS)
            o_ref[rows, :] = x_ref[rows, :] + 0.5 * o_ref[rows, :]
            return carry

        lax.fori_loop(0, n_row_steps, fin, 0)


def _ffn(x, g, wg, wu, wd):
    m, d = x.shape
    fpad = wg.shape[1]
    tm = min(FFN_TM, m)
    tf = FFN_TF
    n_f = fpad // tf
    est = (2 * tm * d * 4) * 2 + tm * d * 2 + 2 * 3 * d * tf * 2
    return pl.pallas_call(
        functools.partial(_ffn_body, n_f=n_f),
        out_shape=jax.ShapeDtypeStruct((m, d), F32),
        grid=(m // tm, n_f),
        in_specs=[
            pl.BlockSpec((tm, d), lambda i, f: (i, 0)),
            pl.BlockSpec((1, d), lambda i, f: (0, 0)),
            pl.BlockSpec((d, tf), lambda i, f: (0, f)),
            pl.BlockSpec((d, tf), lambda i, f: (0, f)),
            pl.BlockSpec((tf, d), lambda i, f: (f, 0)),
        ],
        out_specs=pl.BlockSpec((tm, d), lambda i, f: (i, 0)),
        scratch_shapes=[pltpu.VMEM((tm, d), BF16)],
        compiler_params=_params(("arbitrary", "arbitrary"), est),
        name="ffn",
    )(x, g, wg, wu, wd)


def _proj_body(x_ref, g_ref, w_ref, o_ref, z_ref):
    @pl.when(pl.program_id(1) == 0)
    def _():
        _rmsnorm_rows(x_ref, g_ref, z_ref)

    o_ref[...] = jnp.dot(z_ref[...], w_ref[...], preferred_element_type=F32)


def _proj(x, g, w):
    m, d = x.shape
    n = w.shape[1]
    tm = min(MM_TM, m)
    tn = MM_TN
    est = 2 * tm * d * 4 + tm * d * 2 + 2 * d * tn * 2 + 2 * tm * tn * 4
    return pl.pallas_call(
        _proj_body,
        out_shape=jax.ShapeDtypeStruct((m, n), F32),
        grid=(m // tm, n // tn),
        in_specs=[
            pl.BlockSpec((tm, d), lambda i, j: (i, 0)),
            pl.BlockSpec((1, d), lambda i, j: (0, 0)),
            pl.BlockSpec((d, tn), lambda i, j: (0, j)),
        ],
        out_specs=pl.BlockSpec((tm, tn), lambda i, j: (i, j)),
        scratch_shapes=[pltpu.VMEM((tm, d), BF16)],
        compiler_params=_params(("arbitrary", "arbitrary"), est),
        name="proj",
    )(x, g, w)


def _merge_body(mix_ref, h_ref, w_ref, o_ref):
    o_ref[...] = h_ref[...] + jnp.dot(mix_ref[...], w_ref[...], preferred_element_type=F32)


def _merge(mix, h, w):
    m, d = h.shape
    k = mix.shape[1]
    tm = min(MM_TM, m)
    tn = MM_TN
    est = 2 * tm * k * 2 + 2 * k * tn * 2 + 4 * tm * tn * 4
    return pl.pallas_call(
        _merge_body,
        out_shape=jax.ShapeDtypeStruct((m, d), F32),
        grid=(m // tm, d // tn),
        in_specs=[
            pl.BlockSpec((tm, k), lambda i, j: (i, 0)),
            pl.BlockSpec((tm, tn), lambda i, j: (i, j)),
            pl.BlockSpec((k, tn), lambda i, j: (0, j)),
        ],
        out_specs=pl.BlockSpec((tm, tn), lambda i, j: (i, j)),
        compiler_params=_params(("arbitrary", "arbitrary"), est),
        name="merge",
    )(mix, h, w)


def _ple_body(x_ref, xr_ref, p_ref, g_ref, wg_ref, wp_ref, o_ref, z_ref):
    @pl.when(pl.program_id(1) == 0)
    def _():
        _rmsnorm_rows(x_ref, g_ref, z_ref)

    gate = jax.nn.sigmoid(jnp.dot(z_ref[...], wg_ref[...], preferred_element_type=F32))
    proj = jnp.dot(p_ref[...].astype(BF16), wp_ref[...], preferred_element_type=F32)
    o_ref[...] = xr_ref[...] + gate * proj


def _ple(x, p, g, wg, wp):
    m, d = x.shape
    pd = p.shape[1]
    tm = min(MM_TM, m)
    tn = MM_TN
    est = (2 * tm * d * 4 + tm * d * 2 + 2 * d * tn * 2 + 4 * tm * tn * 4
           + 2 * tm * pd * 4 + 2 * pd * tn * 2)
    return pl.pallas_call(
        _ple_body,
        out_shape=jax.ShapeDtypeStruct((m, d), F32),
        grid=(m // tm, d // tn),
        in_specs=[
            pl.BlockSpec((tm, d), lambda i, j: (i, 0)),
            pl.BlockSpec((tm, tn), lambda i, j: (i, j)),
            pl.BlockSpec((tm, pd), lambda i, j: (i, 0)),
            pl.BlockSpec((1, d), lambda i, j: (0, 0)),
            pl.BlockSpec((d, tn), lambda i, j: (0, j)),
            pl.BlockSpec((pd, tn), lambda i, j: (0, j)),
        ],
        out_specs=pl.BlockSpec((tm, tn), lambda i, j: (i, j)),
        scratch_shapes=[pltpu.VMEM((tm, d), BF16)],
        compiler_params=_params(("arbitrary", "arbitrary"), est),
        name="ple",
    )(x, x, p, g, wg, wp)


def _head_rms(x, g):
    ms = jnp.mean(x * x, axis=-1, keepdims=True)
    return x * lax.rsqrt(ms + EPS) * g


def _attention_chunk(u_ref, kwin_ref, vwin_ref, sink_ref, gq, valid, mix_ref):
    scale = HEAD_DIM ** -0.5
    for k in range(N_KV_HEADS):
        kw = kwin_ref[:, k * HEAD_DIM:(k + 1) * HEAD_DIM].astype(BF16)
        vw = vwin_ref[:, k * HEAD_DIM:(k + 1) * HEAD_DIM].astype(BF16)
        qs = []
        sinks = []
        for g in range(GROUP):
            h = k * GROUP + g
            qs.append(_head_rms(u_ref[:, Q_OFF + h * HEAD_DIM:Q_OFF + (h + 1) * HEAD_DIM], gq))
            sinks.append(jnp.full((CHUNK, 1), sink_ref[h], F32))
        q = jnp.concatenate(qs, axis=0).astype(BF16)
        sink = jnp.concatenate(sinks, axis=0)
        s = lax.dot_general(q, kw, (((1,), (1,)), ((), ())), preferred_element_type=F32) * scale
        if valid is not None:
            s = jnp.where(valid, s, -jnp.inf)
        m = jnp.maximum(jnp.max(s, axis=-1, keepdims=True), sink)
        e = jnp.exp(s - m)
        denom = jnp.sum(e, axis=-1, keepdims=True) + jnp.exp(sink - m)
        p = (e / denom).astype(BF16)
        o = jnp.dot(p, vw, preferred_element_type=F32)
        for g2 in range(GROUP // 2):
            pair = jnp.concatenate([o[(2 * g2) * CHUNK:(2 * g2 + 1) * CHUNK],
                                    o[(2 * g2 + 1) * CHUNK:(2 * g2 + 2) * CHUNK]], axis=-1)
            c0 = (k * GROUP + 2 * g2) * HEAD_DIM
            mix_ref[:, c0:c0 + 2 * HEAD_DIM] = pair.astype(mix_ref.dtype)


def _retention_chunk(u_ref, rope_ref, gret_ref, pos0, load_state, store_state, mix_ref):
    row = lax.broadcasted_iota(jnp.int32, (CHUNK, 1), 0)
    pos = (pos0 + row).astype(F32)
    ang = pos * rope_ref[0:1, :]
    cosv = jnp.cos(ang)
    sinv = jnp.sin(ang) * rope_ref[1:2, :]
    ri = lax.broadcasted_iota(jnp.int32, (CHUNK, CHUNK), 0)
    ci = lax.broadcasted_iota(jnp.int32, (CHUNK, CHUNK), 1)
    dist = jnp.abs(ri - ci).astype(F32)
    rowf = row.astype(F32)

    def rope(x):
        return x * cosv + pltpu.roll(x, RET_DK // 2, 1) * sinv

    for h in range(RET_HEADS):
        lg = LOG_GAMMA[h]
        q = rope(u_ref[:, QR_OFF + h * RET_DK:QR_OFF + (h + 1) * RET_DK])
        kk = rope(u_ref[:, KR_OFF + h * RET_DK:KR_OFF + (h + 1) * RET_DK]) * (RET_DK ** -0.5)
        v = u_ref[:, VR_OFF + h * RET_DV:VR_OFF + (h + 1) * RET_DV].astype(BF16)
        gate = u_ref[:, GR_OFF + h * RET_DV:GR_OFF + (h + 1) * RET_DV]
        state = load_state(h)
        qb = q.astype(BF16)
        sc = lax.dot_general(qb, kk.astype(BF16), (((1,), (1,)), ((), ())),
                             preferred_element_type=F32) * jnp.exp(lg * dist)
        out = jnp.dot(sc.astype(BF16), v, preferred_element_type=F32)
        out = out + (jnp.dot(qb, state.astype(BF16), preferred_element_type=F32)
                     * jnp.exp(lg * (rowf + 1.0)))
        kd = kk * jnp.exp(lg * ((CHUNK - 1.0) - rowf))
        upd = jnp.dot(kd.T.astype(BF16), v, preferred_element_type=F32)
        store_state(h, math.exp(lg * CHUNK) * state + upd)
        mu = jnp.mean(out, axis=-1, keepdims=True)
        cen = out - mu
        var = jnp.mean(cen * cen, axis=-1, keepdims=True)
        normed = cen * lax.rsqrt(var + EPS) * gret_ref[h:h + 1, :]
        ret = normed * (gate * jax.nn.sigmoid(gate))
        c0 = ATTN_WIDTH + h * RET_DV
        mix_ref[:, c0:c0 + RET_DV] = ret.astype(mix_ref.dtype)


def _new_kv(u_ref, gk):
    ks = [_head_rms(u_ref[:, K_OFF + k * HEAD_DIM:K_OFF + (k + 1) * HEAD_DIM], gk)
          for k in range(N_KV_HEADS)]
    return jnp.concatenate(ks, axis=-1), u_ref[:, V_OFF:V_OFF + KV_COLS]


def _mixer_prompt_body(sink_ref, u_ref, gq_ref, gk_ref, gret_ref, rope_ref,
                       mix_ref, ko_ref, vo_ref, st_ref, kwin_ref, vwin_ref, *, n_chunks):
    c = pl.program_id(1)

    @pl.when(c == 0)
    def _():
        kwin_ref[0:WINDOW, :] = jnp.zeros((WINDOW, KV_COLS), F32)
        vwin_ref[0:WINDOW, :] = jnp.zeros((WINDOW, KV_COLS), F32)
        st_ref[...] = jnp.zeros(st_ref.shape, F32)

    knew, vnew = _new_kv(u_ref, gk_ref[...])
    kwin_ref[WINDOW:WINDOW + CHUNK, :] = knew
    vwin_ref[WINDOW:WINDOW + CHUNK, :] = vnew

    key_chunk = lax.broadcasted_iota(jnp.int32, (1, WINDOW + CHUNK), 1) // CHUNK
    valid = (c + key_chunk) >= (WINDOW // CHUNK)
    _attention_chunk(u_ref, kwin_ref, vwin_ref, sink_ref, gq_ref[...], valid, mix_ref)

    def load_state(h):
        return st_ref[0, h]

    def store_state(h, val):
        st_ref[0, h] = val

    _retention_chunk(u_ref, rope_ref, gret_ref, c * CHUNK, load_state, store_state, mix_ref)

    @pl.when(c == n_chunks - 1)
    def _():
        ko_ref[0] = kwin_ref[CHUNK:WINDOW + CHUNK, :]
        vo_ref[0] = vwin_ref[CHUNK:WINDOW + CHUNK, :]

    kshift = kwin_ref[CHUNK:WINDOW + CHUNK, :]
    vshift = vwin_ref[CHUNK:WINDOW + CHUNK, :]
    kwin_ref[0:WINDOW, :] = kshift
    vwin_ref[0:WINDOW, :] = vshift


def _mixer_prompt(u, sinks, gq, gk, gret, rope, batch, seq):
    n_chunks = seq // CHUNK
    est = 2 * CHUNK * IN_WIDTH * 4 + 2 * RET_HEADS * RET_DK * RET_DV * 4 + 8 * 1024 * 1024
    return pl.pallas_call(
        functools.partial(_mixer_prompt_body, n_chunks=n_chunks),
        out_shape=(
            jax.ShapeDtypeStruct((batch * seq, D_MODEL), BF16),
            jax.ShapeDtypeStruct((batch, WINDOW, KV_COLS), F32),
            jax.ShapeDtypeStruct((batch, WINDOW, KV_COLS), F32),
            jax.ShapeDtypeStruct((batch, RET_HEADS, RET_DK, RET_DV), F32),
        ),
        grid=(batch, n_chunks),
        in_specs=[
            pl.BlockSpec(memory_space=pltpu.SMEM),
            pl.BlockSpec((CHUNK, IN_WIDTH), lambda b, c: (b * n_chunks + c, 0)),
            pl.BlockSpec((1, HEAD_DIM), lambda b, c: (0, 0)),
            pl.BlockSpec((1, HEAD_DIM), lambda b, c: (0, 0)),
            pl.BlockSpec((RET_HEADS, RET_DV), lambda b, c: (0, 0)),
            pl.BlockSpec((2, RET_DK), lambda b, c: (0, 0)),
        ],
        out_specs=(
            pl.BlockSpec((CHUNK, D_MODEL), lambda b, c: (b * n_chunks + c, 0)),
            pl.BlockSpec((1, WINDOW, KV_COLS), lambda b, c: (b, 0, 0)),
            pl.BlockSpec((1, WINDOW, KV_COLS), lambda b, c: (b, 0, 0)),
            pl.BlockSpec((1, RET_HEADS, RET_DK, RET_DV), lambda b, c: (b, 0, 0, 0)),
        ),
        scratch_shapes=[pltpu.VMEM((WINDOW + CHUNK, KV_COLS), F32),
                        pltpu.VMEM((WINDOW + CHUNK, KV_COLS), F32)],
        compiler_params=_params(("arbitrary", "arbitrary"), est),
        name="mixer_prompt",
    )(sinks, u, gq, gk, gret, rope)


def _mixer_sample_body(sink_ref, u_ref, ck_ref, cv_ref, sin_ref, gq_ref, gk_ref, gret_ref,
                       rope_ref, mix_ref, ko_ref, vo_ref, st_ref, kwin_ref, vwin_ref):
    knew, vnew = _new_kv(u_ref, gk_ref[...])
    kwin_ref[0:WINDOW, :] = ck_ref[0]
    vwin_ref[0:WINDOW, :] = cv_ref[0]
    kwin_ref[WINDOW:WINDOW + CHUNK, :] = knew
    vwin_ref[WINDOW:WINDOW + CHUNK, :] = vnew
    _attention_chunk(u_ref, kwin_ref, vwin_ref, sink_ref, gq_ref[...], None, mix_ref)

    def load_state(h):
        return sin_ref[0, h]

    def store_state(h, val):
        st_ref[0, h] = val

    _retention_chunk(u_ref, rope_ref, gret_ref, PAST_LEN, load_state, store_state, mix_ref)
    ko_ref[0] = kwin_ref[CHUNK:WINDOW + CHUNK, :]
    vo_ref[0] = vwin_ref[CHUNK:WINDOW + CHUNK, :]


def _mixer_sample(u, cache_k, cache_v, state, sinks, gq, gk, gret, rope):
    nb = cache_k.shape[0]
    est = (2 * CHUNK * IN_WIDTH * 4 + 4 * RET_HEADS * RET_DK * RET_DV * 4 + 8 * 1024 * 1024)
    return pl.pallas_call(
        _mixer_sample_body,
        out_shape=(
            jax.ShapeDtypeStruct((nb * CHUNK, D_MODEL), BF16),
            jax.ShapeDtypeStruct((nb, WINDOW, KV_COLS), F32),
            jax.ShapeDtypeStruct((nb, WINDOW, KV_COLS), F32),
            jax.ShapeDtypeStruct((nb, RET_HEADS, RET_DK, RET_DV), F32),
        ),
        grid=(nb,),
        in_specs=[
            pl.BlockSpec(memory_space=pltpu.SMEM),
            pl.BlockSpec((CHUNK, IN_WIDTH), lambda b: (b, 0)),
            pl.BlockSpec((1, WINDOW, KV_COLS), lambda b: (b, 0, 0)),
            pl.BlockSpec((1, WINDOW, KV_COLS), lambda b: (b, 0, 0)),
            pl.BlockSpec((1, RET_HEADS, RET_DK, RET_DV), lambda b: (b, 0, 0, 0)),
            pl.BlockSpec((1, HEAD_DIM), lambda b: (0, 0)),
            pl.BlockSpec((1, HEAD_DIM), lambda b: (0, 0)),
            pl.BlockSpec((RET_HEADS, RET_DV), lambda b: (0, 0)),
            pl.BlockSpec((2, RET_DK), lambda b: (0, 0)),
        ],
        out_specs=(
            pl.BlockSpec((CHUNK, D_MODEL), lambda b: (b, 0)),
            pl.BlockSpec((1, WINDOW, KV_COLS), lambda b: (b, 0, 0)),
            pl.BlockSpec((1, WINDOW, KV_COLS), lambda b: (b, 0, 0)),
            pl.BlockSpec((1, RET_HEADS, RET_DK, RET_DV), lambda b: (b, 0, 0, 0)),
        ),
        scratch_shapes=[pltpu.VMEM((WINDOW + CHUNK, KV_COLS), F32),
                        pltpu.VMEM((WINDOW + CHUNK, KV_COLS), F32)],
        compiler_params=_params(("arbitrary",), est),
        name="mixer_sample",
    )(sinks, u, cache_k, cache_v, state, gq, gk, gret, rope)


def _rope_table():
    half = RET_DK // 2
    inv_freq = (np.float32(ROPE_BASE) ** (-np.arange(half, dtype=np.float32) / np.float32(half)))
    inv_freq = inv_freq.astype(np.float32)
    sign = np.concatenate([-np.ones(half, np.float32), np.ones(half, np.float32)])
    return jnp.asarray(np.stack([np.concatenate([inv_freq, inv_freq]), sign]))


def _pad_cols(w, mult):
    pad = (-w.shape[1]) % mult
    return jnp.pad(w, ((0, 0), (0, pad))) if pad else w


def _pad_rows(w, mult):
    pad = (-w.shape[0]) % mult
    return jnp.pad(w, ((0, pad), (0, 0))) if pad else w


def kernel(x_prompt, x_sample, cache_attn_k, cache_attn_v, state_ret, p_prompt, p_sample, g_ffn1, w_ffn1_gate, w_ffn1_up, w_ffn1_down, g_mix, w_in, g_q, g_k, attn_sinks, g_ret, w_out, g_ffn2, w_ffn2_gate, w_ffn2_up, w_ffn2_down, g_ple, w_ple_gate, w_ple_proj):
    depth = g_ffn1.shape[0]
    batch, seq, d = x_prompt.shape
    nb, dec_len, _ = x_sample.shape
    assert d == D_MODEL and dec_len == CHUNK and seq % CHUNK == 0
    assert w_in.shape[2] == IN_WIDTH
    rope = _rope_table()

    hp = x_prompt.reshape(batch * seq, d)
    hs = x_sample.reshape(nb * dec_len, d)
    kp_l, vp_l, sp_l, ks_l, vs_l, ss_l = [], [], [], [], [], []
    for i in range(depth):
        wg1 = _pad_cols(w_ffn1_gate[i].astype(BF16), FFN_TF)
        wu1 = _pad_cols(w_ffn1_up[i].astype(BF16), FFN_TF)
        wd1 = _pad_rows(w_ffn1_down[i].astype(BF16), FFN_TF)
        wg2 = _pad_cols(w_ffn2_gate[i].astype(BF16), FFN_TF)
        wu2 = _pad_cols(w_ffn2_up[i].astype(BF16), FFN_TF)
        wd2 = _pad_rows(w_ffn2_down[i].astype(BF16), FFN_TF)
        win = w_in[i].astype(BF16)
        wout = w_out[i].astype(BF16)
        wpg = w_ple_gate[i].astype(BF16)
        wpp = w_ple_proj[i].astype(BF16)
        gf1 = g_ffn1[i].reshape(1, d)
        gf2 = g_ffn2[i].reshape(1, d)
        gmx = g_mix[i].reshape(1, d)
        gpl = g_ple[i].reshape(1, d)
        gq = g_q[i].reshape(1, HEAD_DIM)
        gk = g_k[i].reshape(1, HEAD_DIM)
        sinks = attn_sinks[i]
        gret = g_ret[i]

        hp = _ffn(hp, gf1, wg1, wu1, wd1)
        hs = _ffn(hs, gf1, wg1, wu1, wd1)

        up = _proj(hp, gmx, win)
        us = _proj(hs, gmx, win)
        mixp, kp, vp, sp = _mixer_prompt(up, sinks, gq, gk, gret, rope, batch, seq)
        mixs, ks, vs, ss = _mixer_sample(
            us, cache_attn_k[i].reshape(nb, WINDOW, KV_COLS),
            cache_attn_v[i].reshape(nb, WINDOW, KV_COLS), state_ret[i],
            sinks, gq, gk, gret, rope)
        hp = _merge(mixp, hp, wout)
        hs = _merge(mixs, hs, wout)
        kp_l.append(kp.reshape(batch, WINDOW, N_KV_HEADS, HEAD_DIM))
        vp_l.append(vp.reshape(batch, WINDOW, N_KV_HEADS, HEAD_DIM))
        sp_l.append(sp)
        ks_l.append(ks.reshape(nb, WINDOW, N_KV_HEADS, HEAD_DIM))
        vs_l.append(vs.reshape(nb, WINDOW, N_KV_HEADS, HEAD_DIM))
        ss_l.append(ss)

        hp = _ffn(hp, gf2, wg2, wu2, wd2)
        hs = _ffn(hs, gf2, wg2, wu2, wd2)

        hp = _ple(hp, p_prompt[i].reshape(batch * seq, -1), gpl, wpg, wpp)
        hs = _ple(hs, p_sample[i].reshape(nb * dec_len, -1), gpl, wpg, wpp)

    return (hp.reshape(batch, seq, d), hs.reshape(nb, dec_len, d),
            jnp.stack(kp_l), jnp.stack(vp_l), jnp.stack(sp_l),
            jnp.stack(ks_l), jnp.stack(vs_l), jnp.stack(ss_l))
```

```python
import functools
import math

import numpy as np
import jax
import jax.numpy as jnp
from jax import lax
from jax.experimental import pallas as pl
from jax.experimental.pallas import tpu as pltpu

F32 = jnp.float32
BF16 = jnp.bfloat16

D_MODEL = 4096
CHUNK = 64
HEAD_DIM = 64
N_HEADS = 32
N_KV_HEADS = 4
GROUP = N_HEADS // N_KV_HEADS
WINDOW = 128
RET_HEADS = 8
RET_DK = 128
RET_DV = 256
ATTN_WIDTH = N_HEADS * HEAD_DIM
RET_WIDTH = RET_HEADS * RET_DV
PAST_LEN = 4096
ROPE_BASE = 10000.0
EPS = 1e-6

Q_OFF = 0
K_OFF = Q_OFF + N_HEADS * HEAD_DIM
V_OFF = K_OFF + N_KV_HEADS * HEAD_DIM
QR_OFF = V_OFF + N_KV_HEADS * HEAD_DIM
KR_OFF = QR_OFF + RET_HEADS * RET_DK
VR_OFF = KR_OFF + RET_HEADS * RET_DK
GR_OFF = VR_OFF + RET_WIDTH
IN_WIDTH = GR_OFF + RET_WIDTH
KV_COLS = N_KV_HEADS * HEAD_DIM

LOG_GAMMA = tuple(math.log(1.0 - 2.0 ** (-5.0 - h)) for h in range(RET_HEADS))

V7X_VMEM_BYTES = 64 * 1024 * 1024
VMEM_CAP_BYTES = 60000 * 1024
LANE = 128

FFN_TM = 512
FFN_TF = 256
PROJ_TM, PROJ_TN = 512, 512
MERGE_TM, MERGE_TN = 1024, 1024
PLE_TM, PLE_TN = 512, 1024
NORM_ROWS = 16
SAMPLE_STREAMS = 2


def _vmem_limit(estimate_bytes):
    return int(min(VMEM_CAP_BYTES, max(32 * 1024 * 1024, estimate_bytes * 5 // 4)))


def _params(semantics, vmem_estimate):
    return pltpu.CompilerParams(dimension_semantics=semantics,
                                vmem_limit_bytes=_vmem_limit(vmem_estimate))


def _rmsnorm_rows(x_ref, g_ref, z_ref):
    g = g_ref[...]

    def body(r, carry):
        rows = pl.ds(pl.multiple_of(r * NORM_ROWS, NORM_ROWS), NORM_ROWS)
        x = x_ref[rows, :]
        ms = jnp.mean(x * x, axis=-1, keepdims=True)
        z_ref[rows, :] = (x * lax.rsqrt(ms + EPS) * g).astype(z_ref.dtype)
        return carry

    lax.fori_loop(0, x_ref.shape[0] // NORM_ROWS, body, 0)


def _ffn_body(x_ref, g_ref, wg_ref, wu_ref, wd_ref, o_ref, z_ref, *, n_f):
    f = pl.program_id(1)
    n_row_steps = x_ref.shape[0] // NORM_ROWS

    @pl.when(f == 0)
    def _():
        _rmsnorm_rows(x_ref, g_ref, z_ref)

        def zero(r, carry):
            rows = pl.ds(pl.multiple_of(r * NORM_ROWS, NORM_ROWS), NORM_ROWS)
            o_ref[rows, :] = jnp.zeros((NORM_ROWS, o_ref.shape[1]), F32)
            return carry

        lax.fori_loop(0, n_row_steps, zero, 0)

    z = z_ref[...]
    gate = jnp.dot(z, wg_ref[...], preferred_element_type=F32)
    up = jnp.dot(z, wu_ref[...], preferred_element_type=F32)
    a = (gate * jax.nn.sigmoid(gate) * up).astype(BF16)
    o_ref[...] += jnp.dot(a, wd_ref[...], preferred_element_type=F32)

    @pl.when(f == n_f - 1)
    def _():
        def fin(r, carry):
            rows = pl.ds(pl.multiple_of(r * NORM_ROWS, NORM_ROWS), NORM_ROWS)
            o_ref[rows, :] = x_ref[rows, :] + 0.5 * o_ref[rows, :]
            return carry

        lax.fori_loop(0, n_row_steps, fin, 0)


def _ffn(x, g, wg, wu, wd):
    m, d = x.shape
    fpad = wg.shape[1]
    tm = min(FFN_TM, m)
    tf = FFN_TF
    n_f = fpad // tf
    est = (2 * tm * d * 4) * 2 + tm * d * 2 + 2 * 3 * d * tf * 2
    return pl.pallas_call(
        functools.partial(_ffn_body, n_f=n_f),
        out_shape=jax.ShapeDtypeStruct((m, d), F32),
        grid=(m // tm, n_f),
        in_specs=[
            pl.BlockSpec((tm, d), lambda i, f: (i, 0)),
            pl.BlockSpec((1, d), lambda i, f: (0, 0)),
            pl.BlockSpec((d, tf), lambda i, f: (0, f)),
            pl.BlockSpec((d, tf), lambda i, f: (0, f)),
            pl.BlockSpec((tf, d), lambda i, f: (f, 0)),
        ],
        out_specs=pl.BlockSpec((tm, d), lambda i, f: (i, 0)),
        scratch_shapes=[pltpu.VMEM((tm, d), BF16)],
        compiler_params=_params(("arbitrary", "arbitrary"), est),
        name="ffn",
    )(x, g, wg, wu, wd)


def _proj_body(x_ref, g_ref, w_ref, o_ref, z_ref):
    @pl.when(pl.program_id(1) == 0)
    def _():
        _rmsnorm_rows(x_ref, g_ref, z_ref)

    o_ref[...] = jnp.dot(z_ref[...], w_ref[...], preferred_element_type=F32)


def _proj(x, g, w):
    m, d = x.shape
    n = w.shape[1]
    tm = min(PROJ_TM, m)
    tn = PROJ_TN
    est = 2 * tm * d * 4 + tm * d * 2 + 2 * d * tn * 2 + 2 * tm * tn * 4
    return pl.pallas_call(
        _proj_body,
        out_shape=jax.ShapeDtypeStruct((m, n), F32),
        grid=(m // tm, n // tn),
        in_specs=[
            pl.BlockSpec((tm, d), lambda i, j: (i, 0)),
            pl.BlockSpec((1, d), lambda i, j: (0, 0)),
            pl.BlockSpec((d, tn), lambda i, j: (0, j)),
        ],
        out_specs=pl.BlockSpec((tm, tn), lambda i, j: (i, j)),
        scratch_shapes=[pltpu.VMEM((tm, d), BF16)],
        compiler_params=_params(("arbitrary", "arbitrary"), est),
        name="proj",
    )(x, g, w)


def _merge_body(mix_ref, h_ref, w_ref, o_ref):
    o_ref[...] = h_ref[...] + jnp.dot(mix_ref[...], w_ref[...], preferred_element_type=F32)


def _merge(mix, h, w):
    m, d = h.shape
    k = mix.shape[1]
    tm = min(MERGE_TM, m)
    tn = MERGE_TN
    est = 2 * tm * k * 2 + 2 * k * tn * 2 + 4 * tm * tn * 4
    return pl.pallas_call(
        _merge_body,
        out_shape=jax.ShapeDtypeStruct((m, d), F32),
        grid=(m // tm, d // tn),
        in_specs=[
            pl.BlockSpec((tm, k), lambda i, j: (i, 0)),
            pl.BlockSpec((tm, tn), lambda i, j: (i, j)),
            pl.BlockSpec((k, tn), lambda i, j: (0, j)),
        ],
        out_specs=pl.BlockSpec((tm, tn), lambda i, j: (i, j)),
        compiler_params=_params(("arbitrary", "arbitrary"), est),
        name="merge",
    )(mix, h, w)


def _ple_body(x_ref, xr_ref, p_ref, g_ref, wg_ref, wp_ref, o_ref, z_ref):
    @pl.when(pl.program_id(1) == 0)
    def _():
        _rmsnorm_rows(x_ref, g_ref, z_ref)

    gate = jax.nn.sigmoid(jnp.dot(z_ref[...], wg_ref[...], preferred_element_type=F32))
    proj = jnp.dot(p_ref[...].astype(BF16), wp_ref[...], preferred_element_type=F32)
    o_ref[...] = xr_ref[...] + gate * proj


def _ple(x, p, g, wg, wp):
    m, d = x.shape
    pd = p.shape[1]
    tm = min(PLE_TM, m)
    tn = PLE_TN
    est = (2 * tm * d * 4 + tm * d * 2 + 2 * d * tn * 2 + 4 * tm * tn * 4
           + 2 * tm * pd * 4 + 2 * pd * tn * 2)
    return pl.pallas_call(
        _ple_body,
        out_shape=jax.ShapeDtypeStruct((m, d), F32),
        grid=(m // tm, d // tn),
        in_specs=[
            pl.BlockSpec((tm, d), lambda i, j: (i, 0)),
            pl.BlockSpec((tm, tn), lambda i, j: (i, j)),
            pl.BlockSpec((tm, pd), lambda i, j: (i, 0)),
            pl.BlockSpec((1, d), lambda i, j: (0, 0)),
            pl.BlockSpec((d, tn), lambda i, j: (0, j)),
            pl.BlockSpec((pd, tn), lambda i, j: (0, j)),
        ],
        out_specs=pl.BlockSpec((tm, tn), lambda i, j: (i, j)),
        scratch_shapes=[pltpu.VMEM((tm, d), BF16)],
        compiler_params=_params(("arbitrary", "arbitrary"), est),
        name="ple",
    )(x, x, p, g, wg, wp)


def _rms_last(x, g):
    ms = jnp.mean(x * x, axis=-1, keepdims=True)
    return x * lax.rsqrt(ms + EPS) * g


def _stack_cols(u_ref, n_s, off, width, count):
    return jnp.concatenate(
        [u_ref[s, :, off + j * width:off + (j + 1) * width]
         for s in range(n_s) for j in range(count)], axis=0)


def _bdot_nt(a, b):
    return lax.dot_general(a, b, (((2,), (2,)), ((0,), (0,))), preferred_element_type=F32)


def _bdot_nn(a, b):
    return lax.dot_general(a, b, (((2,), (1,)), ((0,), (0,))), preferred_element_type=F32)


def _new_kv(u_ref, gk, n_s, kwin_ref, vwin_ref):
    nb = n_s * N_KV_HEADS
    knew = _rms_last(_stack_cols(u_ref, n_s, K_OFF, HEAD_DIM, N_KV_HEADS), gk)
    vnew = _stack_cols(u_ref, n_s, V_OFF, HEAD_DIM, N_KV_HEADS)
    kwin_ref[:, WINDOW:WINDOW + CHUNK, :] = knew.reshape(nb, CHUNK, HEAD_DIM)
    vwin_ref[:, WINDOW:WINDOW + CHUNK, :] = vnew.reshape(nb, CHUNK, HEAD_DIM)


def _attention_chunk(u_ref, kwin_ref, vwin_ref, sink_ref, gq, valid, mix_ref, n_s):
    nb = n_s * N_KV_HEADS
    q = _rms_last(_stack_cols(u_ref, n_s, Q_OFF, HEAD_DIM, N_HEADS), gq)
    q = q.astype(BF16).reshape(nb, GROUP * CHUNK, HEAD_DIM)
    s = _bdot_nt(q, kwin_ref[...].astype(BF16)) * (HEAD_DIM ** -0.5)
    if valid is not None:
        s = jnp.where(valid, s, -jnp.inf)
    sink = sink_ref[...]
    m = jnp.maximum(jnp.max(s, axis=-1, keepdims=True), sink)
    e = jnp.exp(s - m)
    denom = jnp.sum(e, axis=-1, keepdims=True) + jnp.exp(sink - m)
    p = (e / denom).astype(BF16)
    o = _bdot_nn(p, vwin_ref[...].astype(BF16))
    for st in range(n_s):
        pieces = [o[st * N_KV_HEADS + k, g * CHUNK:(g + 1) * CHUNK, :]
                  for k in range(N_KV_HEADS) for g in range(GROUP)]
        mix_ref[st, :, 0:ATTN_WIDTH] = jnp.concatenate(pieces, axis=-1).astype(mix_ref.dtype)


def _retention_chunk(u_ref, rope_ref, lg_ref, gret_ref, pos0, state, mix_ref, n_s):
    nb = n_s * RET_HEADS
    row = lax.broadcasted_iota(jnp.int32, (CHUNK, 1), 0)
    pos = (pos0 + row).astype(F32)
    ang = pos * rope_ref[0:1, :]
    cosv = jnp.cos(ang)[None]
    sinv = (jnp.sin(ang) * rope_ref[1:2, :])[None]
    lg = lg_ref[...]
    ri = lax.broadcasted_iota(jnp.int32, (CHUNK, CHUNK), 0)
    ci = lax.broadcasted_iota(jnp.int32, (CHUNK, CHUNK), 1)
    dist = jnp.abs(ri - ci).astype(F32)[None]
    rowf = row.astype(F32)[None]

    def rope(x):
        rolled = pltpu.roll(x, RET_DK // 2, 1)
        return (x.reshape(nb, CHUNK, RET_DK) * cosv
                + rolled.reshape(nb, CHUNK, RET_DK) * sinv)

    q = rope(_stack_cols(u_ref, n_s, QR_OFF, RET_DK, RET_HEADS))
    kk = rope(_stack_cols(u_ref, n_s, KR_OFF, RET_DK, RET_HEADS)) * (RET_DK ** -0.5)
    v = _stack_cols(u_ref, n_s, VR_OFF, RET_DV, RET_HEADS).astype(BF16)
    v = v.reshape(nb, CHUNK, RET_DV)
    qb = q.astype(BF16)
    sc = _bdot_nt(qb, kk.astype(BF16)) * jnp.exp(lg * dist)
    out = _bdot_nn(sc.astype(BF16), v)
    out = out + _bdot_nn(qb, state.astype(BF16)) * jnp.exp(lg * (rowf + 1.0))
    kd = kk * jnp.exp(lg * ((CHUNK - 1.0) - rowf))
    upd = _bdot_nn(jnp.swapaxes(kd, 1, 2).astype(BF16), v)
    new_state = jnp.exp(lg * float(CHUNK)) * state + upd
    mu = jnp.mean(out, axis=-1, keepdims=True)
    cen = out - mu
    var = jnp.mean(cen * cen, axis=-1, keepdims=True)
    normed = cen * lax.rsqrt(var + EPS) * gret_ref[...]
    for st in range(n_s):
        gate = u_ref[st, :, GR_OFF:GR_OFF + RET_WIDTH]
        flat = jnp.concatenate([normed[st * RET_HEADS + h] for h in range(RET_HEADS)], axis=-1)
        ret = flat * (gate * jax.nn.sigmoid(gate))
        mix_ref[st, :, ATTN_WIDTH:ATTN_WIDTH + RET_WIDTH] = ret.astype(mix_ref.dtype)
    return new_state


def _window_to_rows(win_ref, st):
    return jnp.concatenate([win_ref[st * N_KV_HEADS + k, CHUNK:WINDOW + CHUNK, :]
                            for k in range(N_KV_HEADS)], axis=-1)


def _mixer_prompt_body(u_ref, sink_ref, gq_ref, gk_ref, gret_ref, lg_ref, rope_ref,
                       mix_ref, ko_ref, vo_ref, st_ref, kwin_ref, vwin_ref, *, n_chunks, n_s):
    c = pl.program_id(0)
    nbk = n_s * N_KV_HEADS

    @pl.when(c == 0)
    def _():
        kwin_ref[:, 0:WINDOW, :] = jnp.zeros((nbk, WINDOW, HEAD_DIM), F32)
        vwin_ref[:, 0:WINDOW, :] = jnp.zeros((nbk, WINDOW, HEAD_DIM), F32)
        st_ref[...] = jnp.zeros(st_ref.shape, F32)

    _new_kv(u_ref, gk_ref[...], n_s, kwin_ref, vwin_ref)
    key_chunk = lax.broadcasted_iota(jnp.int32, (1, 1, WINDOW + CHUNK), 2) // CHUNK
    valid = (c + key_chunk) >= (WINDOW // CHUNK)
    _attention_chunk(u_ref, kwin_ref, vwin_ref, sink_ref, gq_ref[...], valid, mix_ref, n_s)

    state = st_ref[...].reshape(n_s * RET_HEADS, RET_DK, RET_DV)
    new_state = _retention_chunk(u_ref, rope_ref, lg_ref, gret_ref, c * CHUNK, state, mix_ref, n_s)
    st_ref[...] = new_state.reshape(st_ref.shape)

    @pl.when(c == n_chunks - 1)
    def _():
        for st in range(n_s):
            ko_ref[st] = _window_to_rows(kwin_ref, st)
            vo_ref[st] = _window_to_rows(vwin_ref, st)

    kshift = kwin_ref[:, CHUNK:WINDOW + CHUNK, :]
    vshift = vwin_ref[:, CHUNK:WINDOW + CHUNK, :]
    kwin_ref[:, 0:WINDOW, :] = kshift
    vwin_ref[:, 0:WINDOW, :] = vshift


def _const_spec(shape):
    return pl.BlockSpec(shape, lambda *_: (0,) * len(shape))


def _mixer_prompt(u, sink_col, gq, gk, gret, lg, rope, batch, seq):
    n_chunks = seq // CHUNK
    n_s = batch
    est = (2 * n_s * CHUNK * IN_WIDTH * 4 + 2 * n_s * RET_HEADS * RET_DK * RET_DV * 4
           + 16 * 1024 * 1024)
    return pl.pallas_call(
        functools.partial(_mixer_prompt_body, n_chunks=n_chunks, n_s=n_s),
        out_shape=(
            jax.ShapeDtypeStruct((batch, seq, D_MODEL), BF16),
            jax.ShapeDtypeStruct((batch, WINDOW, KV_COLS), F32),
            jax.ShapeDtypeStruct((batch, WINDOW, KV_COLS), F32),
            jax.ShapeDtypeStruct((batch, RET_HEADS, RET_DK, RET_DV), F32),
        ),
        grid=(n_chunks,),
        in_specs=[
            pl.BlockSpec((n_s, CHUNK, IN_WIDTH), lambda c: (0, c, 0)),
            _const_spec(sink_col.shape),
            _const_spec(gq.shape),
            _const_spec(gk.shape),
            _const_spec(gret.shape),
            _const_spec(lg.shape),
            _const_spec(rope.shape),
        ],
        out_specs=(
            pl.BlockSpec((n_s, CHUNK, D_MODEL), lambda c: (0, c, 0)),
            _const_spec((batch, WINDOW, KV_COLS)),
            _const_spec((batch, WINDOW, KV_COLS)),
            _const_spec((batch, RET_HEADS, RET_DK, RET_DV)),
        ),
        scratch_shapes=[pltpu.VMEM((n_s * N_KV_HEADS, WINDOW + CHUNK, HEAD_DIM), F32),
                        pltpu.VMEM((n_s * N_KV_HEADS, WINDOW + CHUNK, HEAD_DIM), F32)],
        compiler_params=_params(("arbitrary",), est),
        name="mixer_prompt",
    )(u.reshape(batch, seq, IN_WIDTH), sink_col, gq, gk, gret, lg, rope)


def _mixer_sample_body(u_ref, ck_ref, cv_ref, sin_ref, sink_ref, gq_ref, gk_ref, gret_ref,
                       lg_ref, rope_ref, mix_ref, ko_ref, vo_ref, st_ref, kwin_ref, vwin_ref,
                       *, n_s):
    for st in range(n_s):
        for k in range(N_KV_HEADS):
            cols = slice(k * HEAD_DIM, (k + 1) * HEAD_DIM)
            kwin_ref[st * N_KV_HEADS + k, 0:WINDOW, :] = ck_ref[st, :, cols]
            vwin_ref[st * N_KV_HEADS + k, 0:WINDOW, :] = cv_ref[st, :, cols]
    _new_kv(u_ref, gk_ref[...], n_s, kwin_ref, vwin_ref)
    _attention_chunk(u_ref, kwin_ref, vwin_ref, sink_ref, gq_ref[...], None, mix_ref, n_s)
    state = sin_ref[...].reshape(n_s * RET_HEADS, RET_DK, RET_DV)
    new_state = _retention_chunk(u_ref, rope_ref, lg_ref, gret_ref, PAST_LEN, state, mix_ref, n_s)
    st_ref[...] = new_state.reshape(st_ref.shape)
    for st in range(n_s):
        ko_ref[st] = _window_to_rows(kwin_ref, st)
        vo_ref[st] = _window_to_rows(vwin_ref, st)


def _mixer_sample(u, cache_k, cache_v, state, sink_col, gq, gk, gret, lg, rope, n_s):
    nb = cache_k.shape[0]
    est = (2 * n_s * CHUNK * IN_WIDTH * 4 + 4 * n_s * RET_HEADS * RET_DK * RET_DV * 4
           + 16 * 1024 * 1024)
    stream3 = lambda i: (i, 0, 0)
    return pl.pallas_call(
        functools.partial(_mixer_sample_body, n_s=n_s),
        out_shape=(
            jax.ShapeDtypeStruct((nb, CHUNK, D_MODEL), BF16),
            jax.ShapeDtypeStruct((nb, WINDOW, KV_COLS), F32),
            jax.ShapeDtypeStruct((nb, WINDOW, KV_COLS), F32),
            jax.ShapeDtypeStruct((nb, RET_HEADS, RET_DK, RET_DV), F32),
        ),
        grid=(nb // n_s,),
        in_specs=[
            pl.BlockSpec((n_s, CHUNK, IN_WIDTH), stream3),
            pl.BlockSpec((n_s, WINDOW, KV_COLS), stream3),
            pl.BlockSpec((n_s, WINDOW, KV_COLS), stream3),
            pl.BlockSpec((n_s, RET_HEADS, RET_DK, RET_DV), lambda i: (i, 0, 0, 0)),
            _const_spec(sink_col.shape),
            _const_spec(gq.shape),
            _const_spec(gk.shape),
            _const_spec(gret.shape),
            _const_spec(lg.shape),
            _const_spec(rope.shape),
        ],
        out_specs=(
            pl.BlockSpec((n_s, CHUNK, D_MODEL), stream3),
            pl.BlockSpec((n_s, WINDOW, KV_COLS), stream3),
            pl.BlockSpec((n_s, WINDOW, KV_COLS), stream3),
            pl.BlockSpec((n_s, RET_HEADS, RET_DK, RET_DV), lambda i: (i, 0, 0, 0)),
        ),
        scratch_shapes=[pltpu.VMEM((n_s * N_KV_HEADS, WINDOW + CHUNK, HEAD_DIM), F32),
                        pltpu.VMEM((n_s * N_KV_HEADS, WINDOW + CHUNK, HEAD_DIM), F32)],
        compiler_params=_params(("arbitrary",), est),
        name="mixer_sample",
    )(u.reshape(nb, CHUNK, IN_WIDTH), cache_k, cache_v, state, sink_col, gq, gk, gret, lg, rope)


def _mixer_tables(sinks, gret, n_s):
    sink_col = jnp.repeat(sinks.astype(F32), CHUNK).reshape(N_KV_HEADS, GROUP * CHUNK, 1)
    sink_col = jnp.tile(sink_col, (n_s, 1, 1))
    gret3 = jnp.tile(gret.astype(F32).reshape(RET_HEADS, 1, RET_DV), (n_s, 1, 1))
    lg = jnp.tile(jnp.asarray(np.array(LOG_GAMMA, np.float32).reshape(RET_HEADS, 1, 1)),
                  (n_s, 1, 1))
    return sink_col, gret3, lg


def _rope_table():
    half = RET_DK // 2
    inv_freq = (np.float32(ROPE_BASE) ** (-np.arange(half, dtype=np.float32) / np.float32(half)))
    inv_freq = inv_freq.astype(np.float32)
    sign = np.concatenate([-np.ones(half, np.float32), np.ones(half, np.float32)])
    return jnp.asarray(np.stack([np.concatenate([inv_freq, inv_freq]), sign]))


def _pad_cols(w, mult):
    pad = (-w.shape[1]) % mult
    return jnp.pad(w, ((0, 0), (0, pad))) if pad else w


def _pad_rows(w, mult):
    pad = (-w.shape[0]) % mult
    return jnp.pad(w, ((0, pad), (0, 0))) if pad else w


def kernel(x_prompt, x_sample, cache_attn_k, cache_attn_v, state_ret, p_prompt, p_sample, g_ffn1, w_ffn1_gate, w_ffn1_up, w_ffn1_down, g_mix, w_in, g_q, g_k, attn_sinks, g_ret, w_out, g_ffn2, w_ffn2_gate, w_ffn2_up, w_ffn2_down, g_ple, w_ple_gate, w_ple_proj):
    depth = g_ffn1.shape[0]
    batch, seq, d = x_prompt.shape
    nb, dec_len, _ = x_sample.shape
    assert d == D_MODEL and dec_len == CHUNK and seq % CHUNK == 0
    assert w_in.shape[2] == IN_WIDTH and nb % SAMPLE_STREAMS == 0
    rope = _rope_table()

    hp = x_prompt.reshape(batch * seq, d)
    hs = x_sample.reshape(nb * dec_len, d)
    kp_l, vp_l, sp_l, ks_l, vs_l, ss_l = [], [], [], [], [], []
    for i in range(depth):
        wg1 = _pad_cols(w_ffn1_gate[i].astype(BF16), FFN_TF)
        wu1 = _pad_cols(w_ffn1_up[i].astype(BF16), FFN_TF)
        wd1 = _pad_rows(w_ffn1_down[i].astype(BF16), FFN_TF)
        wg2 = _pad_cols(w_ffn2_gate[i].astype(BF16), FFN_TF)
        wu2 = _pad_cols(w_ffn2_up[i].astype(BF16), FFN_TF)
        wd2 = _pad_rows(w_ffn2_down[i].astype(BF16), FFN_TF)
        win = w_in[i].astype(BF16)
        wout = w_out[i].astype(BF16)
        wpg = w_ple_gate[i].astype(BF16)
        wpp = w_ple_proj[i].astype(BF16)
        gf1 = g_ffn1[i].reshape(1, d)
        gf2 = g_ffn2[i].reshape(1, d)
        gmx = g_mix[i].reshape(1, d)
        gpl = g_ple[i].reshape(1, d)
        gq = g_q[i].reshape(1, HEAD_DIM)
        gk = g_k[i].reshape(1, HEAD_DIM)
        sinks = attn_sinks[i]
        gret = g_ret[i]

        hp = _ffn(hp, gf1, wg1, wu1, wd1)
        hs = _ffn(hs, gf1, wg1, wu1, wd1)

        up = _proj(hp, gmx, win)
        us = _proj(hs, gmx, win)
        sink_p, gret_p, lg_p = _mixer_tables(sinks, gret, batch)
        sink_s, gret_s, lg_s = _mixer_tables(sinks, gret, SAMPLE_STREAMS)
        mixp, kp, vp, sp = _mixer_prompt(up, sink_p, gq, gk, gret_p, lg_p, rope, batch, seq)
        mixs, ks, vs, ss = _mixer_sample(
            us, cache_attn_k[i].reshape(nb, WINDOW, KV_COLS),
            cache_attn_v[i].reshape(nb, WINDOW, KV_COLS), state_ret[i],
            sink_s, gq, gk, gret_s, lg_s, rope, SAMPLE_STREAMS)
        mixp = mixp.reshape(batch * seq, d)
        mixs = mixs.reshape(nb * dec_len, d)
        hp = _merge(mixp, hp, wout)
        hs = _merge(mixs, hs, wout)
        kp_l.append(kp.reshape(batch, WINDOW, N_KV_HEADS, HEAD_DIM))
        vp_l.append(vp.reshape(batch, WINDOW, N_KV_HEADS, HEAD_DIM))
        sp_l.append(sp)
        ks_l.append(ks.reshape(nb, WINDOW, N_KV_HEADS, HEAD_DIM))
        vs_l.append(vs.reshape(nb, WINDOW, N_KV_HEADS, HEAD_DIM))
        ss_l.append(ss)

        hp = _ffn(hp, gf2, wg2, wu2, wd2)
        hs = _ffn(hs, gf2, wg2, wu2, wd2)

        hp = _ple(hp, p_prompt[i].reshape(batch * seq, -1), gpl, wpg, wpp)
        hs = _ple(hs, p_sample[i].reshape(nb * dec_len, -1), gpl, wpg, wpp)

    return (hp.reshape(batch, seq, d), hs.reshape(nb, dec_len, d),
            jnp.stack(kp_l), jnp.stack(vp_l), jnp.stack(sp_l),
            jnp.stack(ks_l), jnp.stack(vs_l), jnp.stack(ss_l))
```

```python
import functools
import math

import numpy as np
import jax
import jax.numpy as jnp
from jax import lax
from jax.experimental import pallas as pl
from jax.experimental.pallas import tpu as pltpu

F32 = jnp.float32
BF16 = jnp.bfloat16

D_MODEL = 4096
CHUNK = 64
HEAD_DIM = 64
N_HEADS = 32
N_KV_HEADS = 4
GROUP = N_HEADS // N_KV_HEADS
WINDOW = 128
RET_HEADS = 8
RET_DK = 128
RET_DV = 256
ATTN_WIDTH = N_HEADS * HEAD_DIM
RET_WIDTH = RET_HEADS * RET_DV
PAST_LEN = 4096
ROPE_BASE = 10000.0
EPS = 1e-6

Q_OFF = 0
K_OFF = Q_OFF + N_HEADS * HEAD_DIM
V_OFF = K_OFF + N_KV_HEADS * HEAD_DIM
QR_OFF = V_OFF + N_KV_HEADS * HEAD_DIM
KR_OFF = QR_OFF + RET_HEADS * RET_DK
VR_OFF = KR_OFF + RET_HEADS * RET_DK
GR_OFF = VR_OFF + RET_WIDTH
IN_WIDTH = GR_OFF + RET_WIDTH
KV_COLS = N_KV_HEADS * HEAD_DIM

LOG_GAMMA = tuple(math.log(1.0 - 2.0 ** (-5.0 - h)) for h in range(RET_HEADS))

V7X_VMEM_BYTES = 64 * 1024 * 1024
VMEM_CAP_BYTES = 60000 * 1024

FFN_TM = 1024
FFN_TF = 256
PROJ_TM, PROJ_TN = 1024, 512
MERGE_TM, MERGE_TN = 1024, 1024
PLE_TM, PLE_TN = 1024, 512
NORM_ROWS = 16
SAMPLE_STREAMS = 2
CAST_COLS = (512, 256, 128)


def _vmem_limit(estimate_bytes):
    return int(min(VMEM_CAP_BYTES, max(32 * 1024 * 1024, estimate_bytes * 5 // 4)))


def _params(semantics, vmem_estimate):
    return pltpu.CompilerParams(dimension_semantics=semantics,
                                vmem_limit_bytes=_vmem_limit(vmem_estimate))


def _row_loop(n_rows, body):
    def step(r, carry):
        body(pl.ds(pl.multiple_of(r * NORM_ROWS, NORM_ROWS), NORM_ROWS))
        return carry

    lax.fori_loop(0, n_rows // NORM_ROWS, step, 0)


def _rms_rows(x, g):
    ms = jnp.mean(x * x, axis=-1, keepdims=True)
    return x * lax.rsqrt(ms + EPS) * g


def _cast_plan(shape, n_steps):
    r, c = shape
    bc = next(b for b in CAST_COLS if c % b == 0)
    nc = c // bc
    fits = [nr for nr in range(1, r + 1)
            if r % nr == 0 and (r // nr) % 16 == 0 and nr * nc <= n_steps]
    if not fits:
        return None
    nr = max(fits)
    return r // nr, bc, nr, nc


def _ffn_body(*refs, n_f, tm, n_side):
    x_hbm, g_ref, gn_ref, wg_ref, wu_ref, wd_ref = refs[:6]
    side_in = refs[6:6 + n_side]
    o_ref, zn_ref = refs[6 + n_side:8 + n_side]
    side_out = refs[8 + n_side:8 + 2 * n_side]
    sem = refs[8 + 2 * n_side]
    i = pl.program_id(0)
    f = pl.program_id(1)

    @pl.when(f == 0)
    def _():
        copy = pltpu.make_async_copy(x_hbm.at[pl.ds(pl.multiple_of(i * tm, tm), tm), :],
                                     o_ref, sem)
        copy.start()
        copy.wait()
        g = g_ref[...]

        def init(rows):
            x = o_ref[rows, :]
            zn_ref[rows, :] = _rms_rows(x, g).astype(zn_ref.dtype)
            o_ref[rows, :] = 2.0 * x

        _row_loop(tm, init)

    for src, dst in zip(side_in, side_out):
        dst[...] = src[...].astype(dst.dtype)

    z = zn_ref[...]
    gate = jnp.dot(z, wg_ref[...], preferred_element_type=F32)
    up = jnp.dot(z, wu_ref[...], preferred_element_type=F32)
    a = (gate * jax.nn.sigmoid(gate) * up).astype(BF16)
    o_ref[...] += jnp.dot(a, wd_ref[...], preferred_element_type=F32)

    @pl.when(f == n_f - 1)
    def _():
        gn = gn_ref[...]

        def fin(rows):
            y = 0.5 * o_ref[rows, :]
            o_ref[rows, :] = y
            zn_ref[rows, :] = _rms_rows(y, gn).astype(zn_ref.dtype)

        _row_loop(tm, fin)


def _ffn(x, g, wg, wu, wd, g_next, side=()):
    m, d = x.shape
    tm = min(FFN_TM, m)
    tf = FFN_TF
    n_f = wg.shape[1] // tf
    n_i = m // tm
    plans = [_cast_plan(w.shape, n_i * n_f) for w in side]
    assert all(p is not None for p in plans)

    def side_spec(plan):
        br, bc, nr, nc = plan

        def index(i, f):
            b = jnp.minimum(i * n_f + f, nr * nc - 1)
            return b // nc, b % nc

        return pl.BlockSpec((br, bc), index)

    side_specs = [side_spec(p) for p in plans]
    est = (tm * d * 4 + tm * d * 2 + 2 * 3 * d * tf * 2
           + sum(2 * p[0] * p[1] * 6 for p in plans))
    outs = pl.pallas_call(
        functools.partial(_ffn_body, n_f=n_f, tm=tm, n_side=len(side)),
        out_shape=(jax.ShapeDtypeStruct((m, d), F32), jax.ShapeDtypeStruct((m, d), BF16))
        + tuple(jax.ShapeDtypeStruct(w.shape, BF16) for w in side),
        grid=(n_i, n_f),
        in_specs=[
            pl.BlockSpec(memory_space=pl.ANY),
            pl.BlockSpec((1, d), lambda i, f: (0, 0)),
            pl.BlockSpec((1, d), lambda i, f: (0, 0)),
            pl.BlockSpec((d, tf), lambda i, f: (0, f)),
            pl.BlockSpec((d, tf), lambda i, f: (0, f)),
            pl.BlockSpec((tf, d), lambda i, f: (f, 0)),
        ] + side_specs,
        out_specs=(
            pl.BlockSpec((tm, d), lambda i, f: (i, 0), pipeline_mode=pl.Buffered(1)),
            pl.BlockSpec((tm, d), lambda i, f: (i, 0), pipeline_mode=pl.Buffered(1)),
        ) + tuple(side_specs),
        scratch_shapes=[pltpu.SemaphoreType.DMA(())],
        compiler_params=_params(("arbitrary", "arbitrary"), est),
        name="ffn",
    )(x, g, g_next, wg, wu, wd, *side)
    return outs[0], outs[1], outs[2:]


def _matmul_body(*refs, has_h, has_p):
    z_ref, w_ref = refs[:2]
    o_ref = refs[-1]
    acc = jnp.dot(z_ref[...], w_ref[...], preferred_element_type=F32)
    if has_p:
        h_ref, p_ref, wp_ref = refs[2:5]
        proj = jnp.dot(p_ref[...].astype(BF16), wp_ref[...], preferred_element_type=F32)
        o_ref[...] = h_ref[...] + jax.nn.sigmoid(acc) * proj
    elif has_h:
        o_ref[...] = refs[2][...] + acc
    else:
        o_ref[...] = acc


def _matmul(z, w, tm, tn, h=None, p=None, wp=None, name="matmul"):
    m, k = z.shape
    n = w.shape[1]
    tm = min(tm, m)
    operands = [z, w]
    in_specs = [pl.BlockSpec((tm, k), lambda i, j: (i, 0)),
                pl.BlockSpec((k, tn), lambda i, j: (0, j))]
    est = 2 * tm * k * 2 + 2 * k * tn * 2 + 2 * tm * tn * 4
    if h is not None:
        operands.append(h)
        in_specs.append(pl.BlockSpec((tm, tn), lambda i, j: (i, j)))
        est += 2 * tm * tn * 4
    if p is not None:
        pd = p.shape[1]
        operands += [p, wp]
        in_specs += [pl.BlockSpec((tm, pd), lambda i, j: (i, 0)),
                     pl.BlockSpec((pd, tn), lambda i, j: (0, j))]
        est += 2 * tm * pd * 4 + 2 * pd * tn * 2
    return pl.pallas_call(
        functools.partial(_matmul_body, has_h=h is not None, has_p=p is not None),
        out_shape=jax.ShapeDtypeStruct((m, n), F32),
        grid=(m // tm, n // tn),
        in_specs=in_specs,
        out_specs=pl.BlockSpec((tm, tn), lambda i, j: (i, j)),
        compiler_params=_params(("arbitrary", "arbitrary"), est),
        name=name,
    )(*operands)


def _rms_last(x, g):
    ms = jnp.mean(x * x, axis=-1, keepdims=True)
    return x * lax.rsqrt(ms + EPS) * g


def _stack_cols(u_ref, n_s, off, width, count):
    return jnp.concatenate(
        [u_ref[s, :, off + j * width:off + (j + 1) * width]
         for s in range(n_s) for j in range(count)], axis=0)


def _bdot_nt(a, b):
    return lax.dot_general(a, b, (((2,), (2,)), ((0,), (0,))), preferred_element_type=F32)


def _bdot_nn(a, b):
    return lax.dot_general(a, b, (((2,), (1,)), ((0,), (0,))), preferred_element_type=F32)


def _new_kv(u_ref, gk, n_s, kwin_ref, vwin_ref):
    nb = n_s * N_KV_HEADS
    knew = _rms_last(_stack_cols(u_ref, n_s, K_OFF, HEAD_DIM, N_KV_HEADS), gk)
    vnew = _stack_cols(u_ref, n_s, V_OFF, HEAD_DIM, N_KV_HEADS)
    kwin_ref[:, WINDOW:WINDOW + CHUNK, :] = knew.reshape(nb, CHUNK, HEAD_DIM)
    vwin_ref[:, WINDOW:WINDOW + CHUNK, :] = vnew.reshape(nb, CHUNK, HEAD_DIM)


def _attention_chunk(u_ref, kwin_ref, vwin_ref, sink_ref, gq, valid, mix_ref, n_s):
    nb = n_s * N_KV_HEADS
    q = _rms_last(_stack_cols(u_ref, n_s, Q_OFF, HEAD_DIM, N_HEADS), gq)
    q = q.astype(BF16).reshape(nb, GROUP * CHUNK, HEAD_DIM)
    s = _bdot_nt(q, kwin_ref[...].astype(BF16)) * (HEAD_DIM ** -0.5)
    if valid is not None:
        s = jnp.where(valid, s, -jnp.inf)
    sink = sink_ref[...]
    m = jnp.maximum(jnp.max(s, axis=-1, keepdims=True), sink)
    e = jnp.exp(s - m)
    denom = jnp.sum(e, axis=-1, keepdims=True) + jnp.exp(sink - m)
    p = (e / denom).astype(BF16)
    o = _bdot_nn(p, vwin_ref[...].astype(BF16))
    for st in range(n_s):
        pieces = [o[st * N_KV_HEADS + k, g * CHUNK:(g + 1) * CHUNK, :]
                  for k in range(N_KV_HEADS) for g in range(GROUP)]
        mix_ref[st, :, 0:ATTN_WIDTH] = jnp.concatenate(pieces, axis=-1).astype(mix_ref.dtype)


def _retention_chunk(u_ref, rope_ref, lg_ref, gret_ref, pos0, state, mix_ref, n_s):
    nb = n_s * RET_HEADS
    row = lax.broadcasted_iota(jnp.int32, (CHUNK, 1), 0)
    pos = (pos0 + row).astype(F32)
    ang = pos * rope_ref[0:1, :]
    cosv = jnp.cos(ang)[None]
    sinv = (jnp.sin(ang) * rope_ref[1:2, :])[None]
    lg = lg_ref[...]
    ri = lax.broadcasted_iota(jnp.int32, (CHUNK, CHUNK), 0)
    ci = lax.broadcasted_iota(jnp.int32, (CHUNK, CHUNK), 1)
    dist = jnp.abs(ri - ci).astype(F32)[None]
    rowf = row.astype(F32)[None]

    def rope(x):
        rolled = pltpu.roll(x, RET_DK // 2, 1)
        return (x.reshape(nb, CHUNK, RET_DK) * cosv
                + rolled.reshape(nb, CHUNK, RET_DK) * sinv)

    q = rope(_stack_cols(u_ref, n_s, QR_OFF, RET_DK, RET_HEADS))
    kk = rope(_stack_cols(u_ref, n_s, KR_OFF, RET_DK, RET_HEADS)) * (RET_DK ** -0.5)
    v = _stack_cols(u_ref, n_s, VR_OFF, RET_DV, RET_HEADS).astype(BF16)
    v = v.reshape(nb, CHUNK, RET_DV)
    qb = q.astype(BF16)
    sc = _bdot_nt(qb, kk.astype(BF16)) * jnp.exp(lg * dist)
    out = _bdot_nn(sc.astype(BF16), v)
    out = out + _bdot_nn(qb, state.astype(BF16)) * jnp.exp(lg * (rowf + 1.0))
    kd = kk * jnp.exp(lg * ((CHUNK - 1.0) - rowf))
    upd = _bdot_nn(jnp.swapaxes(kd, 1, 2).astype(BF16), v)
    new_state = jnp.exp(lg * float(CHUNK)) * state + upd
    mu = jnp.mean(out, axis=-1, keepdims=True)
    cen = out - mu
    var = jnp.mean(cen * cen, axis=-1, keepdims=True)
    normed = cen * lax.rsqrt(var + EPS) * gret_ref[...]
    for st in range(n_s):
        gate = u_ref[st, :, GR_OFF:GR_OFF + RET_WIDTH]
        flat = jnp.concatenate([normed[st * RET_HEADS + h] for h in range(RET_HEADS)], axis=-1)
        ret = flat * (gate * jax.nn.sigmoid(gate))
        mix_ref[st, :, ATTN_WIDTH:ATTN_WIDTH + RET_WIDTH] = ret.astype(mix_ref.dtype)
    return new_state


def _window_to_rows(win_ref, st):
    return jnp.concatenate([win_ref[st * N_KV_HEADS + k, CHUNK:WINDOW + CHUNK, :]
                            for k in range(N_KV_HEADS)], axis=-1)


def _mixer_prompt_body(u_ref, sink_ref, gq_ref, gk_ref, gret_ref, lg_ref, rope_ref,
                       mix_ref, ko_ref, vo_ref, st_ref, kwin_ref, vwin_ref, *, n_chunks, n_s):
    c = pl.program_id(0)
    nbk = n_s * N_KV_HEADS

    @pl.when(c == 0)
    def _():
        kwin_ref[:, 0:WINDOW, :] = jnp.zeros((nbk, WINDOW, HEAD_DIM), F32)
        vwin_ref[:, 0:WINDOW, :] = jnp.zeros((nbk, WINDOW, HEAD_DIM), F32)
        st_ref[...] = jnp.zeros(st_ref.shape, F32)

    _new_kv(u_ref, gk_ref[...], n_s, kwin_ref, vwin_ref)
    key_chunk = lax.broadcasted_iota(jnp.int32, (1, 1, WINDOW + CHUNK), 2) // CHUNK
    valid = (c + key_chunk) >= (WINDOW // CHUNK)
    _attention_chunk(u_ref, kwin_ref, vwin_ref, sink_ref, gq_ref[...], valid, mix_ref, n_s)

    state = st_ref[...].reshape(n_s * RET_HEADS, RET_DK, RET_DV)
    new_state = _retention_chunk(u_ref, rope_ref, lg_ref, gret_ref, c * CHUNK, state, mix_ref, n_s)
    st_ref[...] = new_state.reshape(st_ref.shape)

    @pl.when(c == n_chunks - 1)
    def _():
        for st in range(n_s):
            ko_ref[st] = _window_to_rows(kwin_ref, st)
            vo_ref[st] = _window_to_rows(vwin_ref, st)

    kshift = kwin_ref[:, CHUNK:WINDOW + CHUNK, :]
    vshift = vwin_ref[:, CHUNK:WINDOW + CHUNK, :]
    kwin_ref[:, 0:WINDOW, :] = kshift
    vwin_ref[:, 0:WINDOW, :] = vshift


def _const_spec(shape):
    return pl.BlockSpec(shape, lambda *_: (0,) * len(shape))


def _mixer_prompt(u, sink_col, gq, gk, gret, lg, rope, batch, seq):
    n_chunks = seq // CHUNK
    n_s = batch
    est = (2 * n_s * CHUNK * IN_WIDTH * 4 + 2 * n_s * RET_HEADS * RET_DK * RET_DV * 4
           + 16 * 1024 * 1024)
    return pl.pallas_call(
        functools.partial(_mixer_prompt_body, n_chunks=n_chunks, n_s=n_s),
        out_shape=(
            jax.ShapeDtypeStruct((batch, seq, D_MODEL), BF16),
            jax.ShapeDtypeStruct((batch, WINDOW, KV_COLS), F32),
            jax.ShapeDtypeStruct((batch, WINDOW, KV_COLS), F32),
            jax.ShapeDtypeStruct((batch, RET_HEADS, RET_DK, RET_DV), F32),
        ),
        grid=(n_chunks,),
        in_specs=[
            pl.BlockSpec((n_s, CHUNK, IN_WIDTH), lambda c: (0, c, 0)),
            _const_spec(sink_col.shape),
            _const_spec(gq.shape),
            _const_spec(gk.shape),
            _const_spec(gret.shape),
            _const_spec(lg.shape),
            _const_spec(rope.shape),
        ],
        out_specs=(
            pl.BlockSpec((n_s, CHUNK, D_MODEL), lambda c: (0, c, 0)),
            _const_spec((batch, WINDOW, KV_COLS)),
            _const_spec((batch, WINDOW, KV_COLS)),
            _const_spec((batch, RET_HEADS, RET_DK, RET_DV)),
        ),
        scratch_shapes=[pltpu.VMEM((n_s * N_KV_HEADS, WINDOW + CHUNK, HEAD_DIM), F32),
                        pltpu.VMEM((n_s * N_KV_HEADS, WINDOW + CHUNK, HEAD_DIM), F32)],
        compiler_params=_params(("arbitrary",), est),
        name="mixer_prompt",
    )(u.reshape(batch, seq, IN_WIDTH), sink_col, gq, gk, gret, lg, rope)


def _mixer_sample_body(u_ref, ck_ref, cv_ref, sin_ref, sink_ref, gq_ref, gk_ref, gret_ref,
                       lg_ref, rope_ref, mix_ref, ko_ref, vo_ref, st_ref, kwin_ref, vwin_ref,
                       *, n_s):
    for st in range(n_s):
        for k in range(N_KV_HEADS):
            cols = slice(k * HEAD_DIM, (k + 1) * HEAD_DIM)
            kwin_ref[st * N_KV_HEADS + k, 0:WINDOW, :] = ck_ref[st, :, cols]
            vwin_ref[st * N_KV_HEADS + k, 0:WINDOW, :] = cv_ref[st, :, cols]
    _new_kv(u_ref, gk_ref[...], n_s, kwin_ref, vwin_ref)
    _attention_chunk(u_ref, kwin_ref, vwin_ref, sink_ref, gq_ref[...], None, mix_ref, n_s)
    state = sin_ref[...].reshape(n_s * RET_HEADS, RET_DK, RET_DV)
    new_state = _retention_chunk(u_ref, rope_ref, lg_ref, gret_ref, PAST_LEN, state, mix_ref, n_s)
    st_ref[...] = new_state.reshape(st_ref.shape)
    for st in range(n_s):
        ko_ref[st] = _window_to_rows(kwin_ref, st)
        vo_ref[st] = _window_to_rows(vwin_ref, st)


def _mixer_sample(u, cache_k, cache_v, state, sink_col, gq, gk, gret, lg, rope, n_s):
    nb = cache_k.shape[0]
    est = (2 * n_s * CHUNK * IN_WIDTH * 4 + 4 * n_s * RET_HEADS * RET_DK * RET_DV * 4
           + 16 * 1024 * 1024)
    stream3 = lambda i: (i, 0, 0)
    return pl.pallas_call(
        functools.partial(_mixer_sample_body, n_s=n_s),
        out_shape=(
            jax.ShapeDtypeStruct((nb, CHUNK, D_MODEL), BF16),
            jax.ShapeDtypeStruct((nb, WINDOW, KV_COLS), F32),
            jax.ShapeDtypeStruct((nb, WINDOW, KV_COLS), F32),
            jax.ShapeDtypeStruct((nb, RET_HEADS, RET_DK, RET_DV), F32),
        ),
        grid=(nb // n_s,),
        in_specs=[
            pl.BlockSpec((n_s, CHUNK, IN_WIDTH), stream3),
            pl.BlockSpec((n_s, WINDOW, KV_COLS), stream3),
            pl.BlockSpec((n_s, WINDOW, KV_COLS), stream3),
            pl.BlockSpec((n_s, RET_HEADS, RET_DK, RET_DV), lambda i: (i, 0, 0, 0)),
            _const_spec(sink_col.shape),
            _const_spec(gq.shape),
            _const_spec(gk.shape),
            _const_spec(gret.shape),
            _const_spec(lg.shape),
            _const_spec(rope.shape),
        ],
        out_specs=(
            pl.BlockSpec((n_s, CHUNK, D_MODEL), stream3),
            pl.BlockSpec((n_s, WINDOW, KV_COLS), stream3),
            pl.BlockSpec((n_s, WINDOW, KV_COLS), stream3),
            pl.BlockSpec((n_s, RET_HEADS, RET_DK, RET_DV), lambda i: (i, 0, 0, 0)),
        ),
        scratch_shapes=[pltpu.VMEM((n_s * N_KV_HEADS, WINDOW + CHUNK, HEAD_DIM), F32),
                        pltpu.VMEM((n_s * N_KV_HEADS, WINDOW + CHUNK, HEAD_DIM), F32)],
        compiler_params=_params(("arbitrary",), est),
        name="mixer_sample",
    )(u.reshape(nb, CHUNK, IN_WIDTH), cache_k, cache_v, state, sink_col, gq, gk, gret, lg, rope)


def _mixer_tables(sinks, gret, n_s):
    sink_col = jnp.repeat(sinks.astype(F32), CHUNK).reshape(N_KV_HEADS, GROUP * CHUNK, 1)
    sink_col = jnp.tile(sink_col, (n_s, 1, 1))
    gret3 = jnp.tile(gret.astype(F32).reshape(RET_HEADS, 1, RET_DV), (n_s, 1, 1))
    lg = jnp.tile(jnp.asarray(np.array(LOG_GAMMA, np.float32).reshape(RET_HEADS, 1, 1)),
                  (n_s, 1, 1))
    return sink_col, gret3, lg


def _rope_table():
    half = RET_DK // 2
    inv_freq = (np.float32(ROPE_BASE) ** (-np.arange(half, dtype=np.float32) / np.float32(half)))
    inv_freq = inv_freq.astype(np.float32)
    sign = np.concatenate([-np.ones(half, np.float32), np.ones(half, np.float32)])
    return jnp.asarray(np.stack([np.concatenate([inv_freq, inv_freq]), sign]))


def kernel(x_prompt, x_sample, cache_attn_k, cache_attn_v, state_ret, p_prompt, p_sample, g_ffn1, w_ffn1_gate, w_ffn1_up, w_ffn1_down, g_mix, w_in, g_q, g_k, attn_sinks, g_ret, w_out, g_ffn2, w_ffn2_gate, w_ffn2_up, w_ffn2_down, g_ple, w_ple_gate, w_ple_proj):
    depth = g_ffn1.shape[0]
    batch, seq, d = x_prompt.shape
    nb, dec_len, _ = x_sample.shape
    assert d == D_MODEL and dec_len == CHUNK and seq % CHUNK == 0
    assert w_in.shape[2] == IN_WIDTH and nb % SAMPLE_STREAMS == 0
    rope = _rope_table()

    hp = x_prompt.reshape(batch * seq, d)
    hs = x_sample.reshape(nb * dec_len, d)
    kp_l, vp_l, sp_l, ks_l, vs_l, ss_l = [], [], [], [], [], []
    for i in range(depth):
        wg1 = w_ffn1_gate[i].astype(BF16)
        wu1 = w_ffn1_up[i].astype(BF16)
        wd1 = w_ffn1_down[i].astype(BF16)
        gf1 = g_ffn1[i].reshape(1, d)
        gf2 = g_ffn2[i].reshape(1, d)
        gmx = g_mix[i].reshape(1, d)
        gpl = g_ple[i].reshape(1, d)
        gq = g_q[i].reshape(1, HEAD_DIM)
        gk = g_k[i].reshape(1, HEAD_DIM)
        sinks = attn_sinks[i]
        gret = g_ret[i]

        later = (w_ffn2_gate[i], w_ffn2_up[i], w_ffn2_down[i], w_in[i], w_out[i],
                 w_ple_gate[i], w_ple_proj[i])
        hp, zp, (wg2, wu2, wd2, win, wout, wpg, wpp) = _ffn(hp, gf1, wg1, wu1, wd1, gmx, later)
        hs, zs, _ = _ffn(hs, gf1, wg1, wu1, wd1, gmx)

        up = _matmul(zp, win, PROJ_TM, PROJ_TN, name="proj")
        us = _matmul(zs, win, PROJ_TM, PROJ_TN, name="proj")
        sink_p, gret_p, lg_p = _mixer_tables(sinks, gret, batch)
        sink_s, gret_s, lg_s = _mixer_tables(sinks, gret, SAMPLE_STREAMS)
        mixp, kp, vp, sp = _mixer_prompt(up, sink_p, gq, gk, gret_p, lg_p, rope, batch, seq)
        mixs, ks, vs, ss = _mixer_sample(
            us, cache_attn_k[i].reshape(nb, WINDOW, KV_COLS),
            cache_attn_v[i].reshape(nb, WINDOW, KV_COLS), state_ret[i],
            sink_s, gq, gk, gret_s, lg_s, rope, SAMPLE_STREAMS)
        mixp = mixp.reshape(batch * seq, d)
        mixs = mixs.reshape(nb * dec_len, d)
        hp = _matmul(mixp, wout, MERGE_TM, MERGE_TN, h=hp, name="merge")
        hs = _matmul(mixs, wout, MERGE_TM, MERGE_TN, h=hs, name="merge")
        kp_l.append(kp.reshape(batch, WINDOW, N_KV_HEADS, HEAD_DIM))
        vp_l.append(vp.reshape(batch, WINDOW, N_KV_HEADS, HEAD_DIM))
        sp_l.append(sp)
        ks_l.append(ks.reshape(nb, WINDOW, N_KV_HEADS, HEAD_DIM))
        vs_l.append(vs.reshape(nb, WINDOW, N_KV_HEADS, HEAD_DIM))
        ss_l.append(ss)

        hp, zp, _ = _ffn(hp, gf2, wg2, wu2, wd2, gpl)
        hs, zs, _ = _ffn(hs, gf2, wg2, wu2, wd2, gpl)

        hp = _matmul(zp, wpg, PLE_TM, PLE_TN, h=hp, p=p_prompt[i].reshape(batch * seq, -1),
                     wp=wpp, name="ple")
        hs = _matmul(zs, wpg, PLE_TM, PLE_TN, h=hs, p=p_sample[i].reshape(nb * dec_len, -1),
                     wp=wpp, name="ple")

    return (hp.reshape(batch, seq, d), hs.reshape(nb, dec_len, d),
            jnp.stack(kp_l), jnp.stack(vp_l), jnp.stack(sp_l),
            jnp.stack(ks_l), jnp.stack(vs_l), jnp.stack(ss_l))
```

```python
import functools
import math

import numpy as np
import jax
import jax.numpy as jnp
from jax import lax
from jax.experimental import pallas as pl
from jax.experimental.pallas import tpu as pltpu

F32 = jnp.float32
BF16 = jnp.bfloat16

D_MODEL = 4096
CHUNK = 64
HEAD_DIM = 64
N_HEADS = 32
N_KV_HEADS = 4
GROUP = N_HEADS // N_KV_HEADS
WINDOW = 128
RET_HEADS = 8
RET_DK = 128
RET_DV = 256
ATTN_WIDTH = N_HEADS * HEAD_DIM
RET_WIDTH = RET_HEADS * RET_DV
PAST_LEN = 4096
ROPE_BASE = 10000.0
EPS = 1e-6

Q_OFF = 0
K_OFF = Q_OFF + N_HEADS * HEAD_DIM
V_OFF = K_OFF + N_KV_HEADS * HEAD_DIM
QR_OFF = V_OFF + N_KV_HEADS * HEAD_DIM
KR_OFF = QR_OFF + RET_HEADS * RET_DK
VR_OFF = KR_OFF + RET_HEADS * RET_DK
GR_OFF = VR_OFF + RET_WIDTH
IN_WIDTH = GR_OFF + RET_WIDTH
KV_COLS = N_KV_HEADS * HEAD_DIM

LOG_GAMMA = tuple(math.log(1.0 - 2.0 ** (-5.0 - h)) for h in range(RET_HEADS))

V7X_VMEM_BYTES = 64 * 1024 * 1024
VMEM_CAP_BYTES = 60000 * 1024

FFN_TM = 1024
FFN_TF = 256
PROJ_TM, PROJ_TN = 1024, 512
MERGE_TM, MERGE_TN = 1024, 1024
PLE_TM, PLE_TN = 1024, 512
NORM_ROWS = 16
SAMPLE_STREAMS = 2
X_COPY_CHUNKS = 4


def _vmem_limit(estimate_bytes):
    return int(min(VMEM_CAP_BYTES, max(32 * 1024 * 1024, estimate_bytes * 5 // 4)))


def _params(semantics, vmem_estimate):
    return pltpu.CompilerParams(dimension_semantics=semantics,
                                vmem_limit_bytes=_vmem_limit(vmem_estimate))


def _row_loop(n_rows, body):
    def step(r, carry):
        body(pl.ds(pl.multiple_of(r * NORM_ROWS, NORM_ROWS), NORM_ROWS))
        return carry

    lax.fori_loop(0, n_rows // NORM_ROWS, step, 0)


def _rms_rows(x, g):
    ms = jnp.mean(x * x, axis=-1, keepdims=True)
    return x * lax.rsqrt(ms + EPS) * g


def _cast_plan(shape, bc, n_steps):
    r, c = shape
    nc = c // bc
    fits = [nr for nr in range(1, r + 1)
            if r % nr == 0 and (r // nr) % 16 == 0 and nr * nc <= n_steps]
    if not fits:
        return None
    nr = max(fits)
    return r // nr, bc, nr, nc


def _ffn_body(*refs, n_f, tm, n_side):
    x_hbm, g_ref, gn_ref, wg_ref, wu_ref, wd_ref = refs[:6]
    side_in = refs[6:6 + n_side]
    o_ref, zn_ref = refs[6 + n_side:8 + n_side]
    side_out = refs[8 + n_side:8 + 2 * n_side]
    sems = refs[8 + 2 * n_side]
    i = pl.program_id(0)
    f = pl.program_id(1)

    @pl.when(f == 0)
    def _():
        rc = tm // X_COPY_CHUNKS
        copies = [
            pltpu.make_async_copy(
                x_hbm.at[pl.ds(pl.multiple_of(i * tm, tm) + c * rc, rc), :],
                o_ref.at[pl.ds(c * rc, rc), :], sems.at[c])
            for c in range(X_COPY_CHUNKS)]
        for copy in copies:
            copy.start()
        g = g_ref[...]
        for c, copy in enumerate(copies):
            copy.wait()

            def init(rows, base=c * rc):
                rows = pl.ds(rows.start + base, NORM_ROWS)
                x = o_ref[rows, :]
                zn_ref[rows, :] = _rms_rows(x, g).astype(zn_ref.dtype)
                o_ref[rows, :] = 2.0 * x

            _row_loop(rc, init)

    for src, dst in zip(side_in, side_out):
        dst[...] = src[...].astype(dst.dtype)

    z = zn_ref[...]
    gate = jnp.dot(z, wg_ref[...], preferred_element_type=F32)
    up = jnp.dot(z, wu_ref[...], preferred_element_type=F32)
    a = (gate * jax.nn.sigmoid(gate) * up).astype(BF16)
    o_ref[...] += jnp.dot(a, wd_ref[...], preferred_element_type=F32)

    @pl.when(f == n_f - 1)
    def _():
        gn = gn_ref[...]

        def fin(rows):
            y = 0.5 * o_ref[rows, :]
            o_ref[rows, :] = y
            zn_ref[rows, :] = _rms_rows(y, gn).astype(zn_ref.dtype)

        _row_loop(tm, fin)


def _ffn(x, g, wg, wu, wd, g_next, side=()):
    m, d = x.shape
    tm = min(FFN_TM, m)
    n_f, _, tf = wg.shape
    n_i = m // tm
    assert tm % (X_COPY_CHUNKS * NORM_ROWS) == 0
    plans = [_cast_plan(w.shape, bc, n_i * n_f) for w, bc in side]
    assert all(p is not None for p in plans)

    def side_specs(plan):
        br, bc, nr, nc = plan

        def block(i, f):
            b = jnp.minimum(i * n_f + f, nr * nc - 1)
            return b // nc, b % nc

        def src_index(i, f):
            return block(i, f)

        def dst_index(i, f):
            rb, cb = block(i, f)
            return cb, rb, 0

        return pl.BlockSpec((br, bc), src_index), pl.BlockSpec((None, br, bc), dst_index)

    specs = [side_specs(p) for p in plans]
    est = (tm * d * 4 + 2 * tm * d * 2 + 2 * 3 * d * tf * 2
           + sum(2 * p[0] * p[1] * 6 for p in plans))
    outs = pl.pallas_call(
        functools.partial(_ffn_body, n_f=n_f, tm=tm, n_side=len(side)),
        out_shape=(jax.ShapeDtypeStruct((m, d), F32), jax.ShapeDtypeStruct((m, d), BF16))
        + tuple(jax.ShapeDtypeStruct((w.shape[1] // bc, w.shape[0], bc), BF16)
                for w, bc in side),
        grid=(n_i, n_f),
        in_specs=[
            pl.BlockSpec(memory_space=pl.ANY),
            pl.BlockSpec((1, d), lambda i, f: (0, 0)),
            pl.BlockSpec((1, d), lambda i, f: (0, 0)),
            pl.BlockSpec((None, d, tf), lambda i, f: (f, 0, 0)),
            pl.BlockSpec((None, d, tf), lambda i, f: (f, 0, 0)),
            pl.BlockSpec((tf, d), lambda i, f: (f, 0)),
        ] + [src for src, _ in specs],
        out_specs=(
            pl.BlockSpec((tm, d), lambda i, f: (i, 0), pipeline_mode=pl.Buffered(1)),
            pl.BlockSpec((tm, d), lambda i, f: (i, 0)),
        ) + tuple(dst for _, dst in specs),
        scratch_shapes=[pltpu.SemaphoreType.DMA((X_COPY_CHUNKS,))],
        compiler_params=_params(("arbitrary", "arbitrary"), est),
        name="ffn",
    )(x, g, g_next, wg, wu, wd, *[w for w, _ in side])
    return outs[0], outs[1], outs[2:]


def _matmul_body(*refs, has_h, has_p):
    z_ref, w_ref = refs[:2]
    o_ref = refs[-1]
    acc = jnp.dot(z_ref[...], w_ref[...], preferred_element_type=F32)
    if has_p:
        h_ref, p_ref, wp_ref = refs[2:5]
        proj = jnp.dot(p_ref[...].astype(BF16), wp_ref[...], preferred_element_type=F32)
        o_ref[...] = h_ref[...] + jax.nn.sigmoid(acc) * proj
    elif has_h:
        o_ref[...] = refs[2][...] + acc
    else:
        o_ref[...] = acc


def _matmul(z, w, tm, h=None, p=None, wp=None, name="matmul"):
    m, k = z.shape
    n_j, _, tn = w.shape
    n = n_j * tn
    tm = min(tm, m)
    operands = [z, w]
    in_specs = [pl.BlockSpec((tm, k), lambda i, j: (i, 0)),
                pl.BlockSpec((None, k, tn), lambda i, j: (j, 0, 0))]
    est = 2 * tm * k * 2 + 2 * k * tn * 2 + 2 * tm * tn * 4
    if h is not None:
        operands.append(h)
        in_specs.append(pl.BlockSpec((tm, tn), lambda i, j: (i, j)))
        est += 2 * tm * tn * 4
    if p is not None:
        pd = p.shape[1]
        operands += [p, wp]
        in_specs += [pl.BlockSpec((tm, pd), lambda i, j: (i, 0)),
                     pl.BlockSpec((None, pd, tn), lambda i, j: (j, 0, 0))]
        est += 2 * tm * pd * 4 + 2 * pd * tn * 2
    return pl.pallas_call(
        functools.partial(_matmul_body, has_h=h is not None, has_p=p is not None),
        out_shape=jax.ShapeDtypeStruct((m, n), F32),
        grid=(m // tm, n_j),
        in_specs=in_specs,
        out_specs=pl.BlockSpec((tm, tn), lambda i, j: (i, j)),
        compiler_params=_params(("arbitrary", "arbitrary"), est),
        name=name,
    )(*operands)


def _rms_last(x, g):
    ms = jnp.mean(x * x, axis=-1, keepdims=True)
    return x * lax.rsqrt(ms + EPS) * g


def _stack_cols(u_ref, n_s, off, width, count):
    return jnp.concatenate(
        [u_ref[s, :, off + j * width:off + (j + 1) * width]
         for s in range(n_s) for j in range(count)], axis=0)


def _bdot_nt(a, b):
    return lax.dot_general(a, b, (((2,), (2,)), ((0,), (0,))), preferred_element_type=F32)


def _bdot_nn(a, b):
    return lax.dot_general(a, b, (((2,), (1,)), ((0,), (0,))), preferred_element_type=F32)


def _new_kv(u_ref, gk, n_s, kwin_ref, vwin_ref):
    nb = n_s * N_KV_HEADS
    knew = _rms_last(_stack_cols(u_ref, n_s, K_OFF, HEAD_DIM, N_KV_HEADS), gk)
    vnew = _stack_cols(u_ref, n_s, V_OFF, HEAD_DIM, N_KV_HEADS)
    kwin_ref[:, WINDOW:WINDOW + CHUNK, :] = knew.reshape(nb, CHUNK, HEAD_DIM)
    vwin_ref[:, WINDOW:WINDOW + CHUNK, :] = vnew.reshape(nb, CHUNK, HEAD_DIM)


def _attention_chunk(u_ref, kwin_ref, vwin_ref, sink_ref, gq, valid, mix_ref, n_s):
    nb = n_s * N_KV_HEADS
    q = _rms_last(_stack_cols(u_ref, n_s, Q_OFF, HEAD_DIM, N_HEADS), gq)
    q = q.astype(BF16).reshape(nb, GROUP * CHUNK, HEAD_DIM)
    s = _bdot_nt(q, kwin_ref[...].astype(BF16)) * (HEAD_DIM ** -0.5)
    if valid is not None:
        s = jnp.where(valid, s, -jnp.inf)
    sink = sink_ref[...]
    m = jnp.maximum(jnp.max(s, axis=-1, keepdims=True), sink)
    e = jnp.exp(s - m)
    denom = jnp.sum(e, axis=-1, keepdims=True) + jnp.exp(sink - m)
    p = (e / denom).astype(BF16)
    o = _bdot_nn(p, vwin_ref[...].astype(BF16))
    for st in range(n_s):
        pieces = [o[st * N_KV_HEADS + k, g * CHUNK:(g + 1) * CHUNK, :]
                  for k in range(N_KV_HEADS) for g in range(GROUP)]
        mix_ref[st, :, 0:ATTN_WIDTH] = jnp.concatenate(pieces, axis=-1).astype(mix_ref.dtype)


def _retention_chunk(u_ref, rope_ref, lg_ref, gret_ref, pos0, state, mix_ref, n_s):
    nb = n_s * RET_HEADS
    row = lax.broadcasted_iota(jnp.int32, (CHUNK, 1), 0)
    pos = (pos0 + row).astype(F32)
    ang = pos * rope_ref[0:1, :]
    cosv = jnp.cos(ang)[None]
    sinv = (jnp.sin(ang) * rope_ref[1:2, :])[None]
    lg = lg_ref[...]
    ri = lax.broadcasted_iota(jnp.int32, (CHUNK, CHUNK), 0)
    ci = lax.broadcasted_iota(jnp.int32, (CHUNK, CHUNK), 1)
    dist = jnp.abs(ri - ci).astype(F32)[None]
    rowf = row.astype(F32)[None]

    def rope(x):
        rolled = pltpu.roll(x, RET_DK // 2, 1)
        return (x.reshape(nb, CHUNK, RET_DK) * cosv
                + rolled.reshape(nb, CHUNK, RET_DK) * sinv)

    q = rope(_stack_cols(u_ref, n_s, QR_OFF, RET_DK, RET_HEADS))
    kk = rope(_stack_cols(u_ref, n_s, KR_OFF, RET_DK, RET_HEADS)) * (RET_DK ** -0.5)
    v = _stack_cols(u_ref, n_s, VR_OFF, RET_DV, RET_HEADS).astype(BF16)
    v = v.reshape(nb, CHUNK, RET_DV)
    qb = q.astype(BF16)
    sc = _bdot_nt(qb, kk.astype(BF16)) * jnp.exp(lg * dist)
    out = _bdot_nn(sc.astype(BF16), v)
    out = out + _bdot_nn(qb, state.astype(BF16)) * jnp.exp(lg * (rowf + 1.0))
    kd = kk * jnp.exp(lg * ((CHUNK - 1.0) - rowf))
    upd = _bdot_nn(jnp.swapaxes(kd, 1, 2).astype(BF16), v)
    new_state = jnp.exp(lg * float(CHUNK)) * state + upd
    mu = jnp.mean(out, axis=-1, keepdims=True)
    cen = out - mu
    var = jnp.mean(cen * cen, axis=-1, keepdims=True)
    normed = cen * lax.rsqrt(var + EPS) * gret_ref[...]
    for st in range(n_s):
        gate = u_ref[st, :, GR_OFF:GR_OFF + RET_WIDTH]
        flat = jnp.concatenate([normed[st * RET_HEADS + h] for h in range(RET_HEADS)], axis=-1)
        ret = flat * (gate * jax.nn.sigmoid(gate))
        mix_ref[st, :, ATTN_WIDTH:ATTN_WIDTH + RET_WIDTH] = ret.astype(mix_ref.dtype)
    return new_state


def _window_to_rows(win_ref, st):
    return jnp.concatenate([win_ref[st * N_KV_HEADS + k, CHUNK:WINDOW + CHUNK, :]
                            for k in range(N_KV_HEADS)], axis=-1)


def _mixer_prompt_body(u_ref, sink_ref, gq_ref, gk_ref, gret_ref, lg_ref, rope_ref,
                       mix_ref, ko_ref, vo_ref, st_ref, kwin_ref, vwin_ref, *, n_chunks, n_s):
    c = pl.program_id(0)
    nbk = n_s * N_KV_HEADS

    @pl.when(c == 0)
    def _():
        kwin_ref[:, 0:WINDOW, :] = jnp.zeros((nbk, WINDOW, HEAD_DIM), F32)
        vwin_ref[:, 0:WINDOW, :] = jnp.zeros((nbk, WINDOW, HEAD_DIM), F32)
        st_ref[...] = jnp.zeros(st_ref.shape, F32)

    _new_kv(u_ref, gk_ref[...], n_s, kwin_ref, vwin_ref)
    key_chunk = lax.broadcasted_iota(jnp.int32, (1, 1, WINDOW + CHUNK), 2) // CHUNK
    valid = (c + key_chunk) >= (WINDOW // CHUNK)
    _attention_chunk(u_ref, kwin_ref, vwin_ref, sink_ref, gq_ref[...], valid, mix_ref, n_s)

    state = st_ref[...].reshape(n_s * RET_HEADS, RET_DK, RET_DV)
    new_state = _retention_chunk(u_ref, rope_ref, lg_ref, gret_ref, c * CHUNK, state, mix_ref, n_s)
    st_ref[...] = new_state.reshape(st_ref.shape)

    @pl.when(c == n_chunks - 1)
    def _():
        for st in range(n_s):
            ko_ref[st] = _window_to_rows(kwin_ref, st)
            vo_ref[st] = _window_to_rows(vwin_ref, st)

    kshift = kwin_ref[:, CHUNK:WINDOW + CHUNK, :]
    vshift = vwin_ref[:, CHUNK:WINDOW + CHUNK, :]
    kwin_ref[:, 0:WINDOW, :] = kshift
    vwin_ref[:, 0:WINDOW, :] = vshift


def _const_spec(shape):
    return pl.BlockSpec(shape, lambda *_: (0,) * len(shape))


def _mixer_prompt(u, sink_col, gq, gk, gret, lg, rope, batch, seq):
    n_chunks = seq // CHUNK
    n_s = batch
    est = (2 * n_s * CHUNK * IN_WIDTH * 4 + 2 * n_s * RET_HEADS * RET_DK * RET_DV * 4
           + 16 * 1024 * 1024)
    return pl.pallas_call(
        functools.partial(_mixer_prompt_body, n_chunks=n_chunks, n_s=n_s),
        out_shape=(
            jax.ShapeDtypeStruct((batch, seq, D_MODEL), BF16),
            jax.ShapeDtypeStruct((batch, WINDOW, KV_COLS), F32),
            jax.ShapeDtypeStruct((batch, WINDOW, KV_COLS), F32),
            jax.ShapeDtypeStruct((batch, RET_HEADS, RET_DK, RET_DV), F32),
        ),
        grid=(n_chunks,),
        in_specs=[
            pl.BlockSpec((n_s, CHUNK, IN_WIDTH), lambda c: (0, c, 0)),
            _const_spec(sink_col.shape),
            _const_spec(gq.shape),
            _const_spec(gk.shape),
            _const_spec(gret.shape),
            _const_spec(lg.shape),
            _const_spec(rope.shape),
        ],
        out_specs=(
            pl.BlockSpec((n_s, CHUNK, D_MODEL), lambda c: (0, c, 0)),
            _const_spec((batch, WINDOW, KV_COLS)),
            _const_spec((batch, WINDOW, KV_COLS)),
            _const_spec((batch, RET_HEADS, RET_DK, RET_DV)),
        ),
        scratch_shapes=[pltpu.VMEM((n_s * N_KV_HEADS, WINDOW + CHUNK, HEAD_DIM), F32),
                        pltpu.VMEM((n_s * N_KV_HEADS, WINDOW + CHUNK, HEAD_DIM), F32)],
        compiler_params=_params(("arbitrary",), est),
        name="mixer_prompt",
    )(u.reshape(batch, seq, IN_WIDTH), sink_col, gq, gk, gret, lg, rope)


def _mixer_sample_body(u_ref, ck_ref, cv_ref, sin_ref, sink_ref, gq_ref, gk_ref, gret_ref,
                       lg_ref, rope_ref, mix_ref, ko_ref, vo_ref, st_ref, kwin_ref, vwin_ref,
                       *, n_s):
    for st in range(n_s):
        for k in range(N_KV_HEADS):
            cols = slice(k * HEAD_DIM, (k + 1) * HEAD_DIM)
            kwin_ref[st * N_KV_HEADS + k, 0:WINDOW, :] = ck_ref[st, :, cols]
            vwin_ref[st * N_KV_HEADS + k, 0:WINDOW, :] = cv_ref[st, :, cols]
    _new_kv(u_ref, gk_ref[...], n_s, kwin_ref, vwin_ref)
    _attention_chunk(u_ref, kwin_ref, vwin_ref, sink_ref, gq_ref[...], None, mix_ref, n_s)
    state = sin_ref[...].reshape(n_s * RET_HEADS, RET_DK, RET_DV)
    new_state = _retention_chunk(u_ref, rope_ref, lg_ref, gret_ref, PAST_LEN, state, mix_ref, n_s)
    st_ref[...] = new_state.reshape(st_ref.shape)
    for st in range(n_s):
        ko_ref[st] = _window_to_rows(kwin_ref, st)
        vo_ref[st] = _window_to_rows(vwin_ref, st)


def _mixer_sample(u, cache_k, cache_v, state, sink_col, gq, gk, gret, lg, rope, n_s):
    nb = cache_k.shape[0]
    est = (2 * n_s * CHUNK * IN_WIDTH * 4 + 4 * n_s * RET_HEADS * RET_DK * RET_DV * 4
           + 16 * 1024 * 1024)
    stream3 = lambda i: (i, 0, 0)
    return pl.pallas_call(
        functools.partial(_mixer_sample_body, n_s=n_s),
        out_shape=(
            jax.ShapeDtypeStruct((nb, CHUNK, D_MODEL), BF16),
            jax.ShapeDtypeStruct((nb, WINDOW, KV_COLS), F32),
            jax.ShapeDtypeStruct((nb, WINDOW, KV_COLS), F32),
            jax.ShapeDtypeStruct((nb, RET_HEADS, RET_DK, RET_DV), F32),
        ),
        grid=(nb // n_s,),
        in_specs=[
            pl.BlockSpec((n_s, CHUNK, IN_WIDTH), stream3),
            pl.BlockSpec((n_s, WINDOW, KV_COLS), stream3),
            pl.BlockSpec((n_s, WINDOW, KV_COLS), stream3),
            pl.BlockSpec((n_s, RET_HEADS, RET_DK, RET_DV), lambda i: (i, 0, 0, 0)),
            _const_spec(sink_col.shape),
            _const_spec(gq.shape),
            _const_spec(gk.shape),
            _const_spec(gret.shape),
            _const_spec(lg.shape),
            _const_spec(rope.shape),
        ],
        out_specs=(
            pl.BlockSpec((n_s, CHUNK, D_MODEL), stream3),
            pl.BlockSpec((n_s, WINDOW, KV_COLS), stream3),
            pl.BlockSpec((n_s, WINDOW, KV_COLS), stream3),
            pl.BlockSpec((n_s, RET_HEADS, RET_DK, RET_DV), lambda i: (i, 0, 0, 0)),
        ),
        scratch_shapes=[pltpu.VMEM((n_s * N_KV_HEADS, WINDOW + CHUNK, HEAD_DIM), F32),
                        pltpu.VMEM((n_s * N_KV_HEADS, WINDOW + CHUNK, HEAD_DIM), F32)],
        compiler_params=_params(("arbitrary",), est),
        name="mixer_sample",
    )(u.reshape(nb, CHUNK, IN_WIDTH), cache_k, cache_v, state, sink_col, gq, gk, gret, lg, rope)


def _mixer_tables(sinks, gret, n_s):
    sink_col = jnp.repeat(sinks.astype(F32), CHUNK).reshape(N_KV_HEADS, GROUP * CHUNK, 1)
    sink_col = jnp.tile(sink_col, (n_s, 1, 1))
    gret3 = jnp.tile(gret.astype(F32).reshape(RET_HEADS, 1, RET_DV), (n_s, 1, 1))
    lg = jnp.tile(jnp.asarray(np.array(LOG_GAMMA, np.float32).reshape(RET_HEADS, 1, 1)),
                  (n_s, 1, 1))
    return sink_col, gret3, lg


def _rope_table():
    half = RET_DK // 2
    inv_freq = (np.float32(ROPE_BASE) ** (-np.arange(half, dtype=np.float32) / np.float32(half)))
    inv_freq = inv_freq.astype(np.float32)
    sign = np.concatenate([-np.ones(half, np.float32), np.ones(half, np.float32)])
    return jnp.asarray(np.stack([np.concatenate([inv_freq, inv_freq]), sign]))


def _col_tiles(w, tn):
    k, n = w.shape
    return w.reshape(k, n // tn, tn).transpose(1, 0, 2)


def kernel(x_prompt, x_sample, cache_attn_k, cache_attn_v, state_ret, p_prompt, p_sample, g_ffn1, w_ffn1_gate, w_ffn1_up, w_ffn1_down, g_mix, w_in, g_q, g_k, attn_sinks, g_ret, w_out, g_ffn2, w_ffn2_gate, w_ffn2_up, w_ffn2_down, g_ple, w_ple_gate, w_ple_proj):
    depth = g_ffn1.shape[0]
    batch, seq, d = x_prompt.shape
    nb, dec_len, _ = x_sample.shape
    assert d == D_MODEL and dec_len == CHUNK and seq % CHUNK == 0
    assert w_in.shape[2] == IN_WIDTH and nb % SAMPLE_STREAMS == 0
    rope = _rope_table()

    hp = x_prompt.reshape(batch * seq, d)
    hs = x_sample.reshape(nb * dec_len, d)
    kp_l, vp_l, sp_l, ks_l, vs_l, ss_l = [], [], [], [], [], []
    for i in range(depth):
        wg1 = _col_tiles(w_ffn1_gate[i].astype(BF16), FFN_TF)
        wu1 = _col_tiles(w_ffn1_up[i].astype(BF16), FFN_TF)
        wd1 = w_ffn1_down[i].astype(BF16)
        gf1 = g_ffn1[i].reshape(1, d)
        gf2 = g_ffn2[i].reshape(1, d)
        gmx = g_mix[i].reshape(1, d)
        gpl = g_ple[i].reshape(1, d)
        gq = g_q[i].reshape(1, HEAD_DIM)
        gk = g_k[i].reshape(1, HEAD_DIM)
        sinks = attn_sinks[i]
        gret = g_ret[i]

        later = ((w_ffn2_gate[i], FFN_TF), (w_ffn2_up[i], FFN_TF), (w_ffn2_down[i], d),
                 (w_in[i], PROJ_TN), (w_out[i], MERGE_TN), (w_ple_gate[i], PLE_TN),
                 (w_ple_proj[i], PLE_TN))
        hp, zp, (wg2, wu2, wd2, win, wout, wpg, wpp) = _ffn(hp, gf1, wg1, wu1, wd1, gmx, later)
        wd2 = wd2.reshape(wd2.shape[1], d)
        hs, zs, _ = _ffn(hs, gf1, wg1, wu1, wd1, gmx)

        up = _matmul(zp, win, PROJ_TM, name="proj")
        us = _matmul(zs, win, PROJ_TM, name="proj")
        sink_p, gret_p, lg_p = _mixer_tables(sinks, gret, batch)
        sink_s, gret_s, lg_s = _mixer_tables(sinks, gret, SAMPLE_STREAMS)
        mixp, kp, vp, sp = _mixer_prompt(up, sink_p, gq, gk, gret_p, lg_p, rope, batch, seq)
        mixs, ks, vs, ss = _mixer_sample(
            us, cache_attn_k[i].reshape(nb, WINDOW, KV_COLS),
            cache_attn_v[i].reshape(nb, WINDOW, KV_COLS), state_ret[i],
            sink_s, gq, gk, gret_s, lg_s, rope, SAMPLE_STREAMS)
        mixp = mixp.reshape(batch * seq, d)
        mixs = mixs.reshape(nb * dec_len, d)
        hp = _matmul(mixp, wout, MERGE_TM, h=hp, name="merge")
        hs = _matmul(mixs, wout, MERGE_TM, h=hs, name="merge")
        kp_l.append(kp.reshape(batch, WINDOW, N_KV_HEADS, HEAD_DIM))
        vp_l.append(vp.reshape(batch, WINDOW, N_KV_HEADS, HEAD_DIM))
        sp_l.append(sp)
        ks_l.append(ks.reshape(nb, WINDOW, N_KV_HEADS, HEAD_DIM))
        vs_l.append(vs.reshape(nb, WINDOW, N_KV_HEADS, HEAD_DIM))
        ss_l.append(ss)

        hp, zp, _ = _ffn(hp, gf2, wg2, wu2, wd2, gpl)
        hs, zs, _ = _ffn(hs, gf2, wg2, wu2, wd2, gpl)

        hp = _matmul(zp, wpg, PLE_TM, h=hp, p=p_prompt[i].reshape(batch * seq, -1),
                     wp=wpp, name="ple")
        hs = _matmul(zs, wpg, PLE_TM, h=hs, p=p_sample[i].reshape(nb * dec_len, -1),
                     wp=wpp, name="ple")

    return (hp.reshape(batch, seq, d), hs.reshape(nb, dec_len, d),
            jnp.stack(kp_l), jnp.stack(vp_l), jnp.stack(sp_l),
            jnp.stack(ks_l), jnp.stack(vs_l), jnp.stack(ss_l))
```

```python
import functools
import math

import numpy as np
import jax
import jax.numpy as jnp
from jax import lax
from jax.experimental import pallas as pl
from jax.experimental.pallas import tpu as pltpu

F32 = jnp.float32
BF16 = jnp.bfloat16

D_MODEL = 4096
CHUNK = 64
HEAD_DIM = 64
N_HEADS = 32
N_KV_HEADS = 4
GROUP = N_HEADS // N_KV_HEADS
WINDOW = 128
RET_HEADS = 8
RET_DK = 128
RET_DV = 256
ATTN_WIDTH = N_HEADS * HEAD_DIM
RET_WIDTH = RET_HEADS * RET_DV
PAST_LEN = 4096
ROPE_BASE = 10000.0
EPS = 1e-6

Q_OFF = 0
K_OFF = Q_OFF + N_HEADS * HEAD_DIM
V_OFF = K_OFF + N_KV_HEADS * HEAD_DIM
QR_OFF = V_OFF + N_KV_HEADS * HEAD_DIM
KR_OFF = QR_OFF + RET_HEADS * RET_DK
VR_OFF = KR_OFF + RET_HEADS * RET_DK
GR_OFF = VR_OFF + RET_WIDTH
IN_WIDTH = GR_OFF + RET_WIDTH
KV_COLS = N_KV_HEADS * HEAD_DIM

LOG_GAMMA = tuple(math.log(1.0 - 2.0 ** (-5.0 - h)) for h in range(RET_HEADS))

V7X_VMEM_BYTES = 64 * 1024 * 1024
VMEM_CAP_BYTES = 60000 * 1024

FFN_TM = 1024
FFN_TF = 256
PROJ_TM, PROJ_TN = 1024, 512
MERGE_TM, MERGE_TN = 1024, 1024
PLE_TM, PLE_TN = 1024, 512
NORM_ROWS = 16
SAMPLE_STREAMS = 2
ROW_LOOP_UNROLL = 4
X_COPY_CHUNKS = 4


def _vmem_limit(estimate_bytes):
    return int(min(VMEM_CAP_BYTES, max(32 * 1024 * 1024, estimate_bytes * 5 // 4)))


def _params(semantics, vmem_estimate):
    return pltpu.CompilerParams(dimension_semantics=semantics,
                                vmem_limit_bytes=_vmem_limit(vmem_estimate))


def _row_loop(n_rows, body):
    def step(r, carry):
        body(pl.ds(pl.multiple_of(r * NORM_ROWS, NORM_ROWS), NORM_ROWS))
        return carry

    lax.fori_loop(0, n_rows // NORM_ROWS, step, 0, unroll=ROW_LOOP_UNROLL)


def _rms_rows(x, g):
    ms = jnp.mean(x * x, axis=-1, keepdims=True)
    return x * lax.rsqrt(ms + EPS) * g


def _cast_plan(shape, bc, n_steps):
    r, c = shape
    nc = c // bc
    fits = [nr for nr in range(1, r + 1)
            if r % nr == 0 and (r // nr) % 16 == 0 and nr * nc <= n_steps]
    if not fits:
        return None
    nr = max(fits)
    return r // nr, bc, nr, nc


def _ffn_body(*refs, n_f, tm, n_side):
    x_hbm, g_ref, gn_ref, wg_ref, wu_ref, wd_ref = refs[:6]
    side_in = refs[6:6 + n_side]
    o_ref, zn_ref = refs[6 + n_side:8 + n_side]
    side_out = refs[8 + n_side:8 + 2 * n_side]
    sems = refs[8 + 2 * n_side]
    i = pl.program_id(0)
    f = pl.program_id(1)

    @pl.when(f == 0)
    def _():
        rc = tm // X_COPY_CHUNKS
        copies = [
            pltpu.make_async_copy(
                x_hbm.at[pl.ds(pl.multiple_of(i * tm, tm) + c * rc, rc), :],
                o_ref.at[pl.ds(c * rc, rc), :], sems.at[c])
            for c in range(X_COPY_CHUNKS)]
        for copy in copies:
            copy.start()
        g = g_ref[...]
        for c, copy in enumerate(copies):
            copy.wait()

            def init(rows, base=c * rc):
                rows = pl.ds(rows.start + base, NORM_ROWS)
                x = o_ref[rows, :]
                zn_ref[rows, :] = _rms_rows(x, g).astype(zn_ref.dtype)
                o_ref[rows, :] = 2.0 * x

            _row_loop(rc, init)

    for src, dst in zip(side_in, side_out):
        dst[...] = src[...].astype(dst.dtype)

    z = zn_ref[...]
    gate = jnp.dot(z, wg_ref[...], preferred_element_type=F32)
    up = jnp.dot(z, wu_ref[...], preferred_element_type=F32)
    a = (gate * jax.nn.sigmoid(gate) * up).astype(BF16)
    o_ref[...] += jnp.dot(a, wd_ref[...], preferred_element_type=F32)

    @pl.when(f == n_f - 1)
    def _():
        gn = gn_ref[...]

        def fin(rows):
            y = 0.5 * o_ref[rows, :]
            o_ref[rows, :] = y
            zn_ref[rows, :] = _rms_rows(y, gn).astype(zn_ref.dtype)

        _row_loop(tm, fin)


def _ffn(x, g, wg, wu, wd, g_next, side=()):
    m, d = x.shape
    tm = min(FFN_TM, m)
    tile_major = wg.ndim == 3
    tf = wg.shape[2] if tile_major else FFN_TF
    n_f = wg.shape[0] if tile_major else wg.shape[1] // tf
    n_i = m // tm
    if tile_major:
        w_spec = pl.BlockSpec((None, d, tf), lambda i, f: (f, 0, 0))
    else:
        w_spec = pl.BlockSpec((d, tf), lambda i, f: (0, f))
    assert tm % (X_COPY_CHUNKS * NORM_ROWS) == 0
    plans = [_cast_plan(w.shape, bc, n_i * n_f) for w, bc in side]
    assert all(p is not None for p in plans)

    def side_specs(plan):
        br, bc, nr, nc = plan

        def block(i, f):
            b = jnp.minimum(i * n_f + f, nr * nc - 1)
            return b // nc, b % nc

        def src_index(i, f):
            return block(i, f)

        def dst_index(i, f):
            rb, cb = block(i, f)
            return cb, rb, 0

        return pl.BlockSpec((br, bc), src_index), pl.BlockSpec((None, br, bc), dst_index)

    specs = [side_specs(p) for p in plans]
    est = (tm * d * 4 + 2 * tm * d * 2 + 2 * 3 * d * tf * 2
           + sum(2 * p[0] * p[1] * 6 for p in plans))
    outs = pl.pallas_call(
        functools.partial(_ffn_body, n_f=n_f, tm=tm, n_side=len(side)),
        out_shape=(jax.ShapeDtypeStruct((m, d), F32), jax.ShapeDtypeStruct((m, d), BF16))
        + tuple(jax.ShapeDtypeStruct((w.shape[1] // bc, w.shape[0], bc), BF16)
                for w, bc in side),
        grid=(n_i, n_f),
        in_specs=[
            pl.BlockSpec(memory_space=pl.ANY),
            pl.BlockSpec((1, d), lambda i, f: (0, 0)),
            pl.BlockSpec((1, d), lambda i, f: (0, 0)),
            w_spec,
            w_spec,
            pl.BlockSpec((tf, d), lambda i, f: (f, 0)),
        ] + [src for src, _ in specs],
        out_specs=(
            pl.BlockSpec((tm, d), lambda i, f: (i, 0), pipeline_mode=pl.Buffered(1)),
            pl.BlockSpec((tm, d), lambda i, f: (i, 0)),
        ) + tuple(dst for _, dst in specs),
        scratch_shapes=[pltpu.SemaphoreType.DMA((X_COPY_CHUNKS,))],
        compiler_params=_params(("arbitrary", "arbitrary"), est),
        name="ffn",
    )(x, g, g_next, wg, wu, wd, *[w for w, _ in side])
    return outs[0], outs[1], outs[2:]


def _matmul_body(*refs, has_h, has_p):
    z_ref, w_ref = refs[:2]
    o_ref = refs[-1]
    acc = jnp.dot(z_ref[...], w_ref[...], preferred_element_type=F32)
    if has_p:
        h_ref, p_ref, wp_ref = refs[2:5]
        proj = jnp.dot(p_ref[...].astype(BF16), wp_ref[...], preferred_element_type=F32)
        o_ref[...] = h_ref[...] + jax.nn.sigmoid(acc) * proj
    elif has_h:
        o_ref[...] = refs[2][...] + acc
    else:
        o_ref[...] = acc


def _matmul(z, w, tm, h=None, p=None, wp=None, name="matmul"):
    m, k = z.shape
    n_j, _, tn = w.shape
    n = n_j * tn
    tm = min(tm, m)
    operands = [z, w]
    in_specs = [pl.BlockSpec((tm, k), lambda i, j: (i, 0)),
                pl.BlockSpec((None, k, tn), lambda i, j: (j, 0, 0))]
    est = 2 * tm * k * 2 + 2 * k * tn * 2 + 2 * tm * tn * 4
    if h is not None:
        operands.append(h)
        in_specs.append(pl.BlockSpec((tm, tn), lambda i, j: (i, j)))
        est += 2 * tm * tn * 4
    if p is not None:
        pd = p.shape[1]
        operands += [p, wp]
        in_specs += [pl.BlockSpec((tm, pd), lambda i, j: (i, 0)),
                     pl.BlockSpec((None, pd, tn), lambda i, j: (j, 0, 0))]
        est += 2 * tm * pd * 4 + 2 * pd * tn * 2
    return pl.pallas_call(
        functools.partial(_matmul_body, has_h=h is not None, has_p=p is not None),
        out_shape=jax.ShapeDtypeStruct((m, n), F32),
        grid=(m // tm, n_j),
        in_specs=in_specs,
        out_specs=pl.BlockSpec((tm, tn), lambda i, j: (i, j)),
        compiler_params=_params(("arbitrary", "arbitrary"), est),
        name=name,
    )(*operands)


def _rms_last(x, g):
    ms = jnp.mean(x * x, axis=-1, keepdims=True)
    return x * lax.rsqrt(ms + EPS) * g


def _stack_cols(u_ref, n_s, off, width, count):
    return jnp.concatenate(
        [u_ref[s, :, off + j * width:off + (j + 1) * width]
         for s in range(n_s) for j in range(count)], axis=0)


def _bdot_nt(a, b):
    return lax.dot_general(a, b, (((2,), (2,)), ((0,), (0,))), preferred_element_type=F32)


def _bdot_nn(a, b):
    return lax.dot_general(a, b, (((2,), (1,)), ((0,), (0,))), preferred_element_type=F32)


def _new_kv(u_ref, gk, n_s, kwin_ref, vwin_ref):
    nb = n_s * N_KV_HEADS
    knew = _rms_last(_stack_cols(u_ref, n_s, K_OFF, HEAD_DIM, N_KV_HEADS), gk)
    vnew = _stack_cols(u_ref, n_s, V_OFF, HEAD_DIM, N_KV_HEADS)
    kwin_ref[:, WINDOW:WINDOW + CHUNK, :] = knew.reshape(nb, CHUNK, HEAD_DIM)
    vwin_ref[:, WINDOW:WINDOW + CHUNK, :] = vnew.reshape(nb, CHUNK, HEAD_DIM)


def _attention_chunk(u_ref, kwin_ref, vwin_ref, sink_ref, gq, valid, mix_ref, n_s):
    nb = n_s * N_KV_HEADS
    q = _rms_last(_stack_cols(u_ref, n_s, Q_OFF, HEAD_DIM, N_HEADS), gq)
    q = q.astype(BF16).reshape(nb, GROUP * CHUNK, HEAD_DIM)
    s = _bdot_nt(q, kwin_ref[...].astype(BF16)) * (HEAD_DIM ** -0.5)
    if valid is not None:
        s = jnp.where(valid, s, -jnp.inf)
    sink = sink_ref[...]
    m = jnp.maximum(jnp.max(s, axis=-1, keepdims=True), sink)
    e = jnp.exp(s - m)
    denom = jnp.sum(e, axis=-1, keepdims=True) + jnp.exp(sink - m)
    p = (e / denom).astype(BF16)
    o = _bdot_nn(p, vwin_ref[...].astype(BF16))
    for st in range(n_s):
        pieces = [o[st * N_KV_HEADS + k, g * CHUNK:(g + 1) * CHUNK, :]
                  for k in range(N_KV_HEADS) for g in range(GROUP)]
        mix_ref[st, :, 0:ATTN_WIDTH] = jnp.concatenate(pieces, axis=-1).astype(mix_ref.dtype)


def _retention_chunk(u_ref, rope_ref, lg_ref, gret_ref, pos0, state, mix_ref, n_s):
    nb = n_s * RET_HEADS
    row = lax.broadcasted_iota(jnp.int32, (CHUNK, 1), 0)
    pos = (pos0 + row).astype(F32)
    ang = pos * rope_ref[0:1, :]
    cosv = jnp.cos(ang)[None]
    sinv = (jnp.sin(ang) * rope_ref[1:2, :])[None]
    lg = lg_ref[...]
    ri = lax.broadcasted_iota(jnp.int32, (CHUNK, CHUNK), 0)
    ci = lax.broadcasted_iota(jnp.int32, (CHUNK, CHUNK), 1)
    dist = jnp.abs(ri - ci).astype(F32)[None]
    rowf = row.astype(F32)[None]

    def rope(x):
        rolled = pltpu.roll(x, RET_DK // 2, 1)
        return (x.reshape(nb, CHUNK, RET_DK) * cosv
                + rolled.reshape(nb, CHUNK, RET_DK) * sinv)

    q = rope(_stack_cols(u_ref, n_s, QR_OFF, RET_DK, RET_HEADS))
    kk = rope(_stack_cols(u_ref, n_s, KR_OFF, RET_DK, RET_HEADS)) * (RET_DK ** -0.5)
    v = _stack_cols(u_ref, n_s, VR_OFF, RET_DV, RET_HEADS).astype(BF16)
    v = v.reshape(nb, CHUNK, RET_DV)
    qb = q.astype(BF16)
    sc = _bdot_nt(qb, kk.astype(BF16)) * jnp.exp(lg * dist)
    out = _bdot_nn(sc.astype(BF16), v)
    out = out + _bdot_nn(qb, state.astype(BF16)) * jnp.exp(lg * (rowf + 1.0))
    kd = kk * jnp.exp(lg * ((CHUNK - 1.0) - rowf))
    upd = _bdot_nn(jnp.swapaxes(kd, 1, 2).astype(BF16), v)
    new_state = jnp.exp(lg * float(CHUNK)) * state + upd
    mu = jnp.mean(out, axis=-1, keepdims=True)
    cen = out - mu
    var = jnp.mean(cen * cen, axis=-1, keepdims=True)
    normed = cen * lax.rsqrt(var + EPS) * gret_ref[...]
    for st in range(n_s):
        gate = u_ref[st, :, GR_OFF:GR_OFF + RET_WIDTH]
        flat = jnp.concatenate([normed[st * RET_HEADS + h] for h in range(RET_HEADS)], axis=-1)
        ret = flat * (gate * jax.nn.sigmoid(gate))
        mix_ref[st, :, ATTN_WIDTH:ATTN_WIDTH + RET_WIDTH] = ret.astype(mix_ref.dtype)
    return new_state


def _window_to_rows(win_ref, st):
    return jnp.concatenate([win_ref[st * N_KV_HEADS + k, CHUNK:WINDOW + CHUNK, :]
                            for k in range(N_KV_HEADS)], axis=-1)


def _mixer_prompt_body(u_ref, sink_ref, gq_ref, gk_ref, gret_ref, lg_ref, rope_ref,
                       mix_ref, ko_ref, vo_ref, st_ref, kwin_ref, vwin_ref, *, n_chunks, n_s):
    c = pl.program_id(0)
    nbk = n_s * N_KV_HEADS

    @pl.when(c == 0)
    def _():
        kwin_ref[:, 0:WINDOW, :] = jnp.zeros((nbk, WINDOW, HEAD_DIM), F32)
        vwin_ref[:, 0:WINDOW, :] = jnp.zeros((nbk, WINDOW, HEAD_DIM), F32)
        st_ref[...] = jnp.zeros(st_ref.shape, F32)

    _new_kv(u_ref, gk_ref[...], n_s, kwin_ref, vwin_ref)
    key_chunk = lax.broadcasted_iota(jnp.int32, (1, 1, WINDOW + CHUNK), 2) // CHUNK
    valid = (c + key_chunk) >= (WINDOW // CHUNK)
    _attention_chunk(u_ref, kwin_ref, vwin_ref, sink_ref, gq_ref[...], valid, mix_ref, n_s)

    state = st_ref[...].reshape(n_s * RET_HEADS, RET_DK, RET_DV)
    new_state = _retention_chunk(u_ref, rope_ref, lg_ref, gret_ref, c * CHUNK, state, mix_ref, n_s)
    st_ref[...] = new_state.reshape(st_ref.shape)

    @pl.when(c == n_chunks - 1)
    def _():
        for st in range(n_s):
            ko_ref[st] = _window_to_rows(kwin_ref, st)
            vo_ref[st] = _window_to_rows(vwin_ref, st)

    kshift = kwin_ref[:, CHUNK:WINDOW + CHUNK, :]
    vshift = vwin_ref[:, CHUNK:WINDOW + CHUNK, :]
    kwin_ref[:, 0:WINDOW, :] = kshift
    vwin_ref[:, 0:WINDOW, :] = vshift


def _const_spec(shape):
    return pl.BlockSpec(shape, lambda *_: (0,) * len(shape))


def _mixer_prompt(u, sink_col, gq, gk, gret, lg, rope, batch, seq):
    n_chunks = seq // CHUNK
    n_s = batch
    est = (2 * n_s * CHUNK * IN_WIDTH * 4 + 2 * n_s * RET_HEADS * RET_DK * RET_DV * 4
           + 16 * 1024 * 1024)
    return pl.pallas_call(
        functools.partial(_mixer_prompt_body, n_chunks=n_chunks, n_s=n_s),
        out_shape=(
            jax.ShapeDtypeStruct((batch, seq, D_MODEL), BF16),
            jax.ShapeDtypeStruct((batch, WINDOW, KV_COLS), F32),
            jax.ShapeDtypeStruct((batch, WINDOW, KV_COLS), F32),
            jax.ShapeDtypeStruct((batch, RET_HEADS, RET_DK, RET_DV), F32),
        ),
        grid=(n_chunks,),
        in_specs=[
            pl.BlockSpec((n_s, CHUNK, IN_WIDTH), lambda c: (0, c, 0)),
            _const_spec(sink_col.shape),
            _const_spec(gq.shape),
            _const_spec(gk.shape),
            _const_spec(gret.shape),
            _const_spec(lg.shape),
            _const_spec(rope.shape),
        ],
        out_specs=(
            pl.BlockSpec((n_s, CHUNK, D_MODEL), lambda c: (0, c, 0)),
            _const_spec((batch, WINDOW, KV_COLS)),
            _const_spec((batch, WINDOW, KV_COLS)),
            _const_spec((batch, RET_HEADS, RET_DK, RET_DV)),
        ),
        scratch_shapes=[pltpu.VMEM((n_s * N_KV_HEADS, WINDOW + CHUNK, HEAD_DIM), F32),
                        pltpu.VMEM((n_s * N_KV_HEADS, WINDOW + CHUNK, HEAD_DIM), F32)],
        compiler_params=_params(("arbitrary",), est),
        name="mixer_prompt",
    )(u.reshape(batch, seq, IN_WIDTH), sink_col, gq, gk, gret, lg, rope)


def _mixer_sample_body(u_ref, ck_ref, cv_ref, sin_ref, sink_ref, gq_ref, gk_ref, gret_ref,
                       lg_ref, rope_ref, mix_ref, ko_ref, vo_ref, st_ref, kwin_ref, vwin_ref,
                       *, n_s):
    for st in range(n_s):
        for k in range(N_KV_HEADS):
            cols = slice(k * HEAD_DIM, (k + 1) * HEAD_DIM)
            kwin_ref[st * N_KV_HEADS + k, 0:WINDOW, :] = ck_ref[st, :, cols]
            vwin_ref[st * N_KV_HEADS + k, 0:WINDOW, :] = cv_ref[st, :, cols]
    _new_kv(u_ref, gk_ref[...], n_s, kwin_ref, vwin_ref)
    _attention_chunk(u_ref, kwin_ref, vwin_ref, sink_ref, gq_ref[...], None, mix_ref, n_s)
    state = sin_ref[...].reshape(n_s * RET_HEADS, RET_DK, RET_DV)
    new_state = _retention_chunk(u_ref, rope_ref, lg_ref, gret_ref, PAST_LEN, state, mix_ref, n_s)
    st_ref[...] = new_state.reshape(st_ref.shape)
    for st in range(n_s):
        ko_ref[st] = _window_to_rows(kwin_ref, st)
        vo_ref[st] = _window_to_rows(vwin_ref, st)


def _mixer_sample(u, cache_k, cache_v, state, sink_col, gq, gk, gret, lg, rope, n_s):
    nb = cache_k.shape[0]
    est = (2 * n_s * CHUNK * IN_WIDTH * 4 + 4 * n_s * RET_HEADS * RET_DK * RET_DV * 4
           + 16 * 1024 * 1024)
    stream3 = lambda i: (i, 0, 0)
    return pl.pallas_call(
        functools.partial(_mixer_sample_body, n_s=n_s),
        out_shape=(
            jax.ShapeDtypeStruct((nb, CHUNK, D_MODEL), BF16),
            jax.ShapeDtypeStruct((nb, WINDOW, KV_COLS), F32),
            jax.ShapeDtypeStruct((nb, WINDOW, KV_COLS), F32),
            jax.ShapeDtypeStruct((nb, RET_HEADS, RET_DK, RET_DV), F32),
        ),
        grid=(nb // n_s,),
        in_specs=[
            pl.BlockSpec((n_s, CHUNK, IN_WIDTH), stream3),
            pl.BlockSpec((n_s, WINDOW, KV_COLS), stream3),
            pl.BlockSpec((n_s, WINDOW, KV_COLS), stream3),
            pl.BlockSpec((n_s, RET_HEADS, RET_DK, RET_DV), lambda i: (i, 0, 0, 0)),
            _const_spec(sink_col.shape),
            _const_spec(gq.shape),
            _const_spec(gk.shape),
            _const_spec(gret.shape),
            _const_spec(lg.shape),
            _const_spec(rope.shape),
        ],
        out_specs=(
            pl.BlockSpec((n_s, CHUNK, D_MODEL), stream3),
            pl.BlockSpec((n_s, WINDOW, KV_COLS), stream3),
            pl.BlockSpec((n_s, WINDOW, KV_COLS), stream3),
            pl.BlockSpec((n_s, RET_HEADS, RET_DK, RET_DV), lambda i: (i, 0, 0, 0)),
        ),
        scratch_shapes=[pltpu.VMEM((n_s * N_KV_HEADS, WINDOW + CHUNK, HEAD_DIM), F32),
                        pltpu.VMEM((n_s * N_KV_HEADS, WINDOW + CHUNK, HEAD_DIM), F32)],
        compiler_params=_params(("arbitrary",), est),
        name="mixer_sample",
    )(u.reshape(nb, CHUNK, IN_WIDTH), cache_k, cache_v, state, sink_col, gq, gk, gret, lg, rope)


def _mixer_tables(sinks, gret, n_s):
    sink_col = jnp.repeat(sinks.astype(F32), CHUNK).reshape(N_KV_HEADS, GROUP * CHUNK, 1)
    sink_col = jnp.tile(sink_col, (n_s, 1, 1))
    gret3 = jnp.tile(gret.astype(F32).reshape(RET_HEADS, 1, RET_DV), (n_s, 1, 1))
    lg = jnp.tile(jnp.asarray(np.array(LOG_GAMMA, np.float32).reshape(RET_HEADS, 1, 1)),
                  (n_s, 1, 1))
    return sink_col, gret3, lg


def _rope_table():
    half = RET_DK // 2
    inv_freq = (np.float32(ROPE_BASE) ** (-np.arange(half, dtype=np.float32) / np.float32(half)))
    inv_freq = inv_freq.astype(np.float32)
    sign = np.concatenate([-np.ones(half, np.float32), np.ones(half, np.float32)])
    return jnp.asarray(np.stack([np.concatenate([inv_freq, inv_freq]), sign]))


def kernel(x_prompt, x_sample, cache_attn_k, cache_attn_v, state_ret, p_prompt, p_sample, g_ffn1, w_ffn1_gate, w_ffn1_up, w_ffn1_down, g_mix, w_in, g_q, g_k, attn_sinks, g_ret, w_out, g_ffn2, w_ffn2_gate, w_ffn2_up, w_ffn2_down, g_ple, w_ple_gate, w_ple_proj):
    depth = g_ffn1.shape[0]
    batch, seq, d = x_prompt.shape
    nb, dec_len, _ = x_sample.shape
    assert d == D_MODEL and dec_len == CHUNK and seq % CHUNK == 0
    assert w_in.shape[2] == IN_WIDTH and nb % SAMPLE_STREAMS == 0
    rope = _rope_table()

    hp = x_prompt.reshape(batch * seq, d)
    hs = x_sample.reshape(nb * dec_len, d)
    kp_l, vp_l, sp_l, ks_l, vs_l, ss_l = [], [], [], [], [], []
    for i in range(depth):
        wg1 = w_ffn1_gate[i].astype(BF16)
        wu1 = w_ffn1_up[i].astype(BF16)
        wd1 = w_ffn1_down[i].astype(BF16)
        gf1 = g_ffn1[i].reshape(1, d)
        gf2 = g_ffn2[i].reshape(1, d)
        gmx = g_mix[i].reshape(1, d)
        gpl = g_ple[i].reshape(1, d)
        gq = g_q[i].reshape(1, HEAD_DIM)
        gk = g_k[i].reshape(1, HEAD_DIM)
        sinks = attn_sinks[i]
        gret = g_ret[i]

        later = ((w_ffn2_gate[i], FFN_TF), (w_ffn2_up[i], FFN_TF), (w_ffn2_down[i], d),
                 (w_in[i], PROJ_TN), (w_out[i], MERGE_TN), (w_ple_gate[i], PLE_TN),
                 (w_ple_proj[i], PLE_TN))
        hp, zp, (wg2, wu2, wd2, win, wout, wpg, wpp) = _ffn(hp, gf1, wg1, wu1, wd1, gmx, later)
        wd2 = wd2.reshape(wd2.shape[1], d)
        hs, zs, _ = _ffn(hs, gf1, wg1, wu1, wd1, gmx)

        up = _matmul(zp, win, PROJ_TM, name="proj")
        us = _matmul(zs, win, PROJ_TM, name="proj")
        sink_p, gret_p, lg_p = _mixer_tables(sinks, gret, batch)
        sink_s, gret_s, lg_s = _mixer_tables(sinks, gret, SAMPLE_STREAMS)
        mixp, kp, vp, sp = _mixer_prompt(up, sink_p, gq, gk, gret_p, lg_p, rope, batch, seq)
        mixs, ks, vs, ss = _mixer_sample(
            us, cache_attn_k[i].reshape(nb, WINDOW, KV_COLS),
            cache_attn_v[i].reshape(nb, WINDOW, KV_COLS), state_ret[i],
            sink_s, gq, gk, gret_s, lg_s, rope, SAMPLE_STREAMS)
        mixp = mixp.reshape(batch * seq, d)
        mixs = mixs.reshape(nb * dec_len, d)
        hp = _matmul(mixp, wout, MERGE_TM, h=hp, name="merge")
        hs = _matmul(mixs, wout, MERGE_TM, h=hs, name="merge")
        kp_l.append(kp.reshape(batch, WINDOW, N_KV_HEADS, HEAD_DIM))
        vp_l.append(vp.reshape(batch, WINDOW, N_KV_HEADS, HEAD_DIM))
        sp_l.append(sp)
        ks_l.append(ks.reshape(nb, WINDOW, N_KV_HEADS, HEAD_DIM))
        vs_l.append(vs.reshape(nb, WINDOW, N_KV_HEADS, HEAD_DIM))
        ss_l.append(ss)

        hp, zp, _ = _ffn(hp, gf2, wg2, wu2, wd2, gpl)
        hs, zs, _ = _ffn(hs, gf2, wg2, wu2, wd2, gpl)

        hp = _matmul(zp, wpg, PLE_TM, h=hp, p=p_prompt[i].reshape(batch * seq, -1),
                     wp=wpp, name="ple")
        hs = _matmul(zs, wpg, PLE_TM, h=hs, p=p_sample[i].reshape(nb * dec_len, -1),
                     wp=wpp, name="ple")

    return (hp.reshape(batch, seq, d), hs.reshape(nb, dec_len, d),
            jnp.stack(kp_l), jnp.stack(vp_l), jnp.stack(sp_l),
            jnp.stack(ks_l), jnp.stack(vs_l), jnp.stack(ss_l))
```

```python
import functools
import math

import numpy as np
import jax
import jax.numpy as jnp
from jax import lax
from jax.experimental import pallas as pl
from jax.experimental.pallas import tpu as pltpu

F32 = jnp.float32
BF16 = jnp.bfloat16

D_MODEL = 4096
CHUNK = 64
HEAD_DIM = 64
N_HEADS = 32
N_KV_HEADS = 4
GROUP = N_HEADS // N_KV_HEADS
WINDOW = 128
RET_HEADS = 8
RET_DK = 128
RET_DV = 256
ATTN_WIDTH = N_HEADS * HEAD_DIM
RET_WIDTH = RET_HEADS * RET_DV
PAST_LEN = 4096
ROPE_BASE = 10000.0
EPS = 1e-6

Q_OFF = 0
K_OFF = Q_OFF + N_HEADS * HEAD_DIM
V_OFF = K_OFF + N_KV_HEADS * HEAD_DIM
QR_OFF = V_OFF + N_KV_HEADS * HEAD_DIM
KR_OFF = QR_OFF + RET_HEADS * RET_DK
VR_OFF = KR_OFF + RET_HEADS * RET_DK
GR_OFF = VR_OFF + RET_WIDTH
IN_WIDTH = GR_OFF + RET_WIDTH
KV_COLS = N_KV_HEADS * HEAD_DIM

LOG2_E = math.log2(math.e)
LOG_GAMMA = tuple(math.log(1.0 - 2.0 ** (-5.0 - h)) for h in range(RET_HEADS))

V7X_VMEM_BYTES = 64 * 1024 * 1024
VMEM_CAP_BYTES = 60000 * 1024

FFN_TM = 1024
FFN_TF = 256
PROJ_TM, PROJ_TN = 1024, 512
MERGE_TM, MERGE_TN = 1024, 1024
PLE_TM, PLE_TN = 1024, 512
NORM_ROWS = 16
SAMPLE_STREAMS = 2
ROW_LOOP_UNROLL = 4
X_COPY_CHUNKS = 4


def _vmem_limit(estimate_bytes):
    return int(min(VMEM_CAP_BYTES, max(32 * 1024 * 1024, estimate_bytes * 5 // 4)))


def _params(semantics, vmem_estimate):
    return pltpu.CompilerParams(dimension_semantics=semantics,
                                vmem_limit_bytes=_vmem_limit(vmem_estimate))


def _row_loop(n_rows, body):
    def step(r, carry):
        body(pl.ds(pl.multiple_of(r * NORM_ROWS, NORM_ROWS), NORM_ROWS))
        return carry

    lax.fori_loop(0, n_rows // NORM_ROWS, step, 0, unroll=ROW_LOOP_UNROLL)


def _rms_rows(x, g):
    ms = jnp.mean(x * x, axis=-1, keepdims=True)
    return x * lax.rsqrt(ms + EPS) * g


def _cast_plan(shape, bc, n_steps):
    r, c = shape
    nc = c // bc
    fits = [nr for nr in range(1, r + 1)
            if r % nr == 0 and (r // nr) % 16 == 0 and nr * nc <= n_steps]
    if not fits:
        return None
    nr = max(fits)
    return r // nr, bc, nr, nc


def _ffn_body(*refs, n_i, n_f, tm, n_side):
    x_hbm, g_ref, gn_ref, wg_ref, wu_ref, wd_ref = refs[:6]
    side_in = refs[6:6 + n_side]
    o_hbm, zn_ref = refs[6 + n_side:8 + n_side]
    side_out = refs[8 + n_side:8 + 2 * n_side]
    o_ref, x_sems, o_sems = refs[8 + 2 * n_side:]
    i = pl.program_id(0)
    f = pl.program_id(1)
    rc = tm // X_COPY_CHUNKS

    def x_copy(c):
        return pltpu.make_async_copy(
            x_hbm.at[pl.ds(pl.multiple_of(i * tm, tm) + c * rc, rc), :],
            o_ref.at[pl.ds(c * rc, rc), :], x_sems.at[c])

    def o_copy(c, tile):
        return pltpu.make_async_copy(
            o_ref.at[pl.ds(c * rc, rc), :],
            o_hbm.at[pl.ds(pl.multiple_of(tile * tm, tm) + c * rc, rc), :], o_sems.at[c])

    @pl.when(f == 0)
    def _():
        for c in range(X_COPY_CHUNKS):
            @pl.when(i > 0)
            def _(c=c):
                o_copy(c, i - 1).wait()

            x_copy(c).start()
        g = g_ref[...]
        for c in range(X_COPY_CHUNKS):
            x_copy(c).wait()

            def init(rows, base=c * rc):
                rows = pl.ds(rows.start + base, NORM_ROWS)
                zn_ref[rows, :] = _rms_rows(o_ref[rows, :], g).astype(zn_ref.dtype)

            _row_loop(rc, init)

    for src, dst in zip(side_in, side_out):
        dst[...] = src[...].astype(dst.dtype)

    z = zn_ref[...]
    gate = jnp.dot(z, wg_ref[...], preferred_element_type=F32)
    up = jnp.dot(z, wu_ref[...], preferred_element_type=F32)
    a = (0.5 * gate * jax.nn.sigmoid(gate) * up).astype(BF16)
    o_ref[...] += jnp.dot(a, wd_ref[...], preferred_element_type=F32)

    @pl.when(f == n_f - 1)
    def _():
        for c in range(X_COPY_CHUNKS):
            o_copy(c, i).start()
        gn = gn_ref[...]

        def fin(rows):
            zn_ref[rows, :] = _rms_rows(o_ref[rows, :], gn).astype(zn_ref.dtype)

        _row_loop(tm, fin)

        @pl.when(i == n_i - 1)
        def _():
            for c in range(X_COPY_CHUNKS):
                o_copy(c, i).wait()


def _ffn(x, g, wg, wu, wd, g_next, side=()):
    m, d = x.shape
    tm = min(FFN_TM, m)
    tile_major = wg.ndim == 3
    tf = wg.shape[2] if tile_major else FFN_TF
    n_f = wg.shape[0] if tile_major else wg.shape[1] // tf
    n_i = m // tm
    if tile_major:
        w_spec = pl.BlockSpec((None, d, tf), lambda i, f: (f, 0, 0))
    else:
        w_spec = pl.BlockSpec((d, tf), lambda i, f: (0, f))
    assert tm % (X_COPY_CHUNKS * NORM_ROWS) == 0
    plans = [_cast_plan(w.shape, bc, n_i * n_f) for w, bc in side]
    assert all(p is not None for p in plans)

    def side_specs(plan):
        br, bc, nr, nc = plan

        def block(i, f):
            b = jnp.minimum(i * n_f + f, nr * nc - 1)
            return b // nc, b % nc

        def src_index(i, f):
            return block(i, f)

        def dst_index(i, f):
            rb, cb = block(i, f)
            return cb, rb, 0

        return pl.BlockSpec((br, bc), src_index), pl.BlockSpec((None, br, bc), dst_index)

    specs = [side_specs(p) for p in plans]
    est = (tm * d * 4 + 2 * tm * d * 2 + 2 * 3 * d * tf * 2
           + sum(2 * p[0] * p[1] * 6 for p in plans))
    outs = pl.pallas_call(
        functools.partial(_ffn_body, n_i=n_i, n_f=n_f, tm=tm, n_side=len(side)),
        out_shape=(jax.ShapeDtypeStruct((m, d), F32), jax.ShapeDtypeStruct((m, d), BF16))
        + tuple(jax.ShapeDtypeStruct((w.shape[1] // bc, w.shape[0], bc), BF16)
                for w, bc in side),
        grid=(n_i, n_f),
        in_specs=[
            pl.BlockSpec(memory_space=pl.ANY),
            pl.BlockSpec((1, d), lambda i, f: (0, 0)),
            pl.BlockSpec((1, d), lambda i, f: (0, 0)),
            w_spec,
            w_spec,
            pl.BlockSpec((tf, d), lambda i, f: (f, 0)),
        ] + [src for src, _ in specs],
        out_specs=(
            pl.BlockSpec(memory_space=pl.ANY),
            pl.BlockSpec((tm, d), lambda i, f: (i, 0)),
        ) + tuple(dst for _, dst in specs),
        scratch_shapes=[pltpu.VMEM((tm, d), F32),
                        pltpu.SemaphoreType.DMA((X_COPY_CHUNKS,)),
                        pltpu.SemaphoreType.DMA((X_COPY_CHUNKS,))],
        compiler_params=_params(("arbitrary", "arbitrary"), est),
        name="ffn",
    )(x, g, g_next, wg, wu, wd, *[w for w, _ in side])
    return outs[0], outs[1], outs[2:]


def _matmul_body(*refs, has_h, has_p):
    z_ref, w_ref = refs[:2]
    o_ref = refs[-1]
    acc = jnp.dot(z_ref[...], w_ref[...], preferred_element_type=F32)
    if has_p:
        h_ref, p_ref, wp_ref = refs[2:5]
        proj = jnp.dot(p_ref[...].astype(BF16), wp_ref[...], preferred_element_type=F32)
        o_ref[...] = h_ref[...] + jax.nn.sigmoid(acc) * proj
    elif has_h:
        o_ref[...] = refs[2][...] + acc
    else:
        o_ref[...] = acc


def _matmul(z, w, tm, h=None, p=None, wp=None, name="matmul"):
    m, k = z.shape
    n_j, _, tn = w.shape
    n = n_j * tn
    tm = min(tm, m)
    operands = [z, w]
    in_specs = [pl.BlockSpec((tm, k), lambda i, j: (i, 0)),
                pl.BlockSpec((None, k, tn), lambda i, j: (j, 0, 0))]
    est = 2 * tm * k * 2 + 2 * k * tn * 2 + 2 * tm * tn * 4
    if h is not None:
        operands.append(h)
        in_specs.append(pl.BlockSpec((tm, tn), lambda i, j: (i, j)))
        est += 2 * tm * tn * 4
    if p is not None:
        pd = p.shape[1]
        operands += [p, wp]
        in_specs += [pl.BlockSpec((tm, pd), lambda i, j: (i, 0)),
                     pl.BlockSpec((None, pd, tn), lambda i, j: (j, 0, 0))]
        est += 2 * tm * pd * 4 + 2 * pd * tn * 2
    return pl.pallas_call(
        functools.partial(_matmul_body, has_h=h is not None, has_p=p is not None),
        out_shape=jax.ShapeDtypeStruct((m, n), F32),
        grid=(m // tm, n_j),
        in_specs=in_specs,
        out_specs=pl.BlockSpec((tm, tn), lambda i, j: (i, j)),
        compiler_params=_params(("arbitrary", "arbitrary"), est),
        name=name,
    )(*operands)


def _rms_last(x, g):
    ms = jnp.mean(x * x, axis=-1, keepdims=True)
    return x * lax.rsqrt(ms + EPS) * g


def _stack_cols(u_ref, n_s, off, width, count):
    return jnp.concatenate(
        [u_ref[s, :, off + j * width:off + (j + 1) * width]
         for s in range(n_s) for j in range(count)], axis=0)


def _bdot_nt(a, b):
    return lax.dot_general(a, b, (((2,), (2,)), ((0,), (0,))), preferred_element_type=F32)


def _bdot_nn(a, b):
    return lax.dot_general(a, b, (((2,), (1,)), ((0,), (0,))), preferred_element_type=F32)


def _new_kv(u_ref, gk, n_s, kwin_ref, vwin_ref):
    nb = n_s * N_KV_HEADS
    knew = _rms_last(_stack_cols(u_ref, n_s, K_OFF, HEAD_DIM, N_KV_HEADS), gk)
    vnew = _stack_cols(u_ref, n_s, V_OFF, HEAD_DIM, N_KV_HEADS)
    kwin_ref[:, WINDOW:WINDOW + CHUNK, :] = knew.reshape(nb, CHUNK, HEAD_DIM)
    vwin_ref[:, WINDOW:WINDOW + CHUNK, :] = vnew.reshape(nb, CHUNK, HEAD_DIM)


def _attention_chunk(u_ref, kwin_ref, vwin_ref, sink_ref, gq, valid, mix_ref, n_s):
    nb = n_s * N_KV_HEADS
    q = _rms_last(_stack_cols(u_ref, n_s, Q_OFF, HEAD_DIM, N_HEADS), gq)
    q = q.astype(BF16).reshape(nb, GROUP * CHUNK, HEAD_DIM)
    s = _bdot_nt(q, kwin_ref[...].astype(BF16)) * (HEAD_DIM ** -0.5 * LOG2_E)
    if valid is not None:
        s = jnp.where(valid, s, -jnp.inf)
    sink = sink_ref[...] * LOG2_E
    m = jnp.maximum(jnp.max(s, axis=-1, keepdims=True), sink)
    e = jnp.exp2(s - m)
    denom = jnp.sum(e, axis=-1, keepdims=True) + jnp.exp2(sink - m)
    p = (e / denom).astype(BF16)
    o = _bdot_nn(p, vwin_ref[...].astype(BF16))
    for st in range(n_s):
        pieces = [o[st * N_KV_HEADS + k, g * CHUNK:(g + 1) * CHUNK, :]
                  for k in range(N_KV_HEADS) for g in range(GROUP)]
        mix_ref[st, :, 0:ATTN_WIDTH] = jnp.concatenate(pieces, axis=-1).astype(mix_ref.dtype)


def _retention_chunk(u_ref, rope_ref, lg_ref, gret_ref, pos0, state, mix_ref, n_s):
    nb = n_s * RET_HEADS
    row = lax.broadcasted_iota(jnp.int32, (CHUNK, 1), 0)
    pos = (pos0 + row).astype(F32)
    ang = pos * rope_ref[0:1, :]
    cosv = jnp.cos(ang)[None]
    sinv = (jnp.sin(ang) * rope_ref[1:2, :])[None]
    lg = lg_ref[...]
    ri = lax.broadcasted_iota(jnp.int32, (CHUNK, CHUNK), 0)
    ci = lax.broadcasted_iota(jnp.int32, (CHUNK, CHUNK), 1)
    dist = jnp.abs(ri - ci).astype(F32)[None]
    rowf = row.astype(F32)[None]

    def rope(x):
        rolled = pltpu.roll(x, RET_DK // 2, 1)
        return (x.reshape(nb, CHUNK, RET_DK) * cosv
                + rolled.reshape(nb, CHUNK, RET_DK) * sinv)

    q = rope(_stack_cols(u_ref, n_s, QR_OFF, RET_DK, RET_HEADS))
    kk = rope(_stack_cols(u_ref, n_s, KR_OFF, RET_DK, RET_HEADS)) * (RET_DK ** -0.5)
    v = _stack_cols(u_ref, n_s, VR_OFF, RET_DV, RET_HEADS).astype(BF16)
    v = v.reshape(nb, CHUNK, RET_DV)
    qb = q.astype(BF16)
    sc = _bdot_nt(qb, kk.astype(BF16)) * jnp.exp(lg * dist)
    out = _bdot_nn(sc.astype(BF16), v)
    out = out + _bdot_nn(qb, state.astype(BF16)) * jnp.exp(lg * (rowf + 1.0))
    kd = kk * jnp.exp(lg * ((CHUNK - 1.0) - rowf))
    upd = _bdot_nn(jnp.swapaxes(kd, 1, 2).astype(BF16), v)
    new_state = jnp.exp(lg * float(CHUNK)) * state + upd
    mu = jnp.mean(out, axis=-1, keepdims=True)
    cen = out - mu
    var = jnp.mean(cen * cen, axis=-1, keepdims=True)
    normed = cen * lax.rsqrt(var + EPS) * gret_ref[...]
    for st in range(n_s):
        gate = u_ref[st, :, GR_OFF:GR_OFF + RET_WIDTH]
        flat = jnp.concatenate([normed[st * RET_HEADS + h] for h in range(RET_HEADS)], axis=-1)
        ret = flat * (gate * jax.nn.sigmoid(gate))
        mix_ref[st, :, ATTN_WIDTH:ATTN_WIDTH + RET_WIDTH] = ret.astype(mix_ref.dtype)
    return new_state


def _window_to_rows(win_ref, st):
    return jnp.concatenate([win_ref[st * N_KV_HEADS + k, CHUNK:WINDOW + CHUNK, :]
                            for k in range(N_KV_HEADS)], axis=-1)


def _mixer_prompt_body(u_ref, sink_ref, gq_ref, gk_ref, gret_ref, lg_ref, rope_ref,
                       mix_ref, ko_ref, vo_ref, st_ref, kwin_ref, vwin_ref, *, n_chunks, n_s):
    c = pl.program_id(0)
    nbk = n_s * N_KV_HEADS

    @pl.when(c == 0)
    def _():
        kwin_ref[:, 0:WINDOW, :] = jnp.zeros((nbk, WINDOW, HEAD_DIM), F32)
        vwin_ref[:, 0:WINDOW, :] = jnp.zeros((nbk, WINDOW, HEAD_DIM), F32)
        st_ref[...] = jnp.zeros(st_ref.shape, F32)

    _new_kv(u_ref, gk_ref[...], n_s, kwin_ref, vwin_ref)
    key_chunk = lax.broadcasted_iota(jnp.int32, (1, 1, WINDOW + CHUNK), 2) // CHUNK
    valid = (c + key_chunk) >= (WINDOW // CHUNK)
    _attention_chunk(u_ref, kwin_ref, vwin_ref, sink_ref, gq_ref[...], valid, mix_ref, n_s)

    state = st_ref[...].reshape(n_s * RET_HEADS, RET_DK, RET_DV)
    new_state = _retention_chunk(u_ref, rope_ref, lg_ref, gret_ref, c * CHUNK, state, mix_ref, n_s)
    st_ref[...] = new_state.reshape(st_ref.shape)

    @pl.when(c == n_chunks - 1)
    def _():
        for st in range(n_s):
            ko_ref[st] = _window_to_rows(kwin_ref, st)
            vo_ref[st] = _window_to_rows(vwin_ref, st)

    kshift = kwin_ref[:, CHUNK:WINDOW + CHUNK, :]
    vshift = vwin_ref[:, CHUNK:WINDOW + CHUNK, :]
    kwin_ref[:, 0:WINDOW, :] = kshift
    vwin_ref[:, 0:WINDOW, :] = vshift


def _const_spec(shape):
    return pl.BlockSpec(shape, lambda *_: (0,) * len(shape))


def _mixer_prompt(u, sink_col, gq, gk, gret, lg, rope, batch, seq):
    n_chunks = seq // CHUNK
    n_s = batch
    est = (2 * n_s * CHUNK * IN_WIDTH * 4 + 2 * n_s * RET_HEADS * RET_DK * RET_DV * 4
           + 16 * 1024 * 1024)
    return pl.pallas_call(
        functools.partial(_mixer_prompt_body, n_chunks=n_chunks, n_s=n_s),
        out_shape=(
            jax.ShapeDtypeStruct((batch, seq, D_MODEL), BF16),
            jax.ShapeDtypeStruct((batch, WINDOW, KV_COLS), F32),
            jax.ShapeDtypeStruct((batch, WINDOW, KV_COLS), F32),
            jax.ShapeDtypeStruct((batch, RET_HEADS, RET_DK, RET_DV), F32),
        ),
        grid=(n_chunks,),
        in_specs=[
            pl.BlockSpec((n_s, CHUNK, IN_WIDTH), lambda c: (0, c, 0)),
            _const_spec(sink_col.shape),
            _const_spec(gq.shape),
            _const_spec(gk.shape),
            _const_spec(gret.shape),
            _const_spec(lg.shape),
            _const_spec(rope.shape),
        ],
        out_specs=(
            pl.BlockSpec((n_s, CHUNK, D_MODEL), lambda c: (0, c, 0)),
            _const_spec((batch, WINDOW, KV_COLS)),
            _const_spec((batch, WINDOW, KV_COLS)),
            _const_spec((batch, RET_HEADS, RET_DK, RET_DV)),
        ),
        scratch_shapes=[pltpu.VMEM((n_s * N_KV_HEADS, WINDOW + CHUNK, HEAD_DIM), F32),
                        pltpu.VMEM((n_s * N_KV_HEADS, WINDOW + CHUNK, HEAD_DIM), F32)],
        compiler_params=_params(("arbitrary",), est),
        name="mixer_prompt",
    )(u.reshape(batch, seq, IN_WIDTH), sink_col, gq, gk, gret, lg, rope)


def _mixer_sample_body(u_ref, ck_ref, cv_ref, sin_ref, sink_ref, gq_ref, gk_ref, gret_ref,
                       lg_ref, rope_ref, mix_ref, ko_ref, vo_ref, st_ref, kwin_ref, vwin_ref,
                       *, n_s):
    for st in range(n_s):
        for k in range(N_KV_HEADS):
            cols = slice(k * HEAD_DIM, (k + 1) * HEAD_DIM)
            kwin_ref[st * N_KV_HEADS + k, 0:WINDOW, :] = ck_ref[st, :, cols]
            vwin_ref[st * N_KV_HEADS + k, 0:WINDOW, :] = cv_ref[st, :, cols]
    _new_kv(u_ref, gk_ref[...], n_s, kwin_ref, vwin_ref)
    _attention_chunk(u_ref, kwin_ref, vwin_ref, sink_ref, gq_ref[...], None, mix_ref, n_s)
    state = sin_ref[...].reshape(n_s * RET_HEADS, RET_DK, RET_DV)
    new_state = _retention_chunk(u_ref, rope_ref, lg_ref, gret_ref, PAST_LEN, state, mix_ref, n_s)
    st_ref[...] = new_state.reshape(st_ref.shape)
    for st in range(n_s):
        ko_ref[st] = _window_to_rows(kwin_ref, st)
        vo_ref[st] = _window_to_rows(vwin_ref, st)


def _mixer_sample(u, cache_k, cache_v, state, sink_col, gq, gk, gret, lg, rope, n_s):
    nb = cache_k.shape[0]
    est = (2 * n_s * CHUNK * IN_WIDTH * 4 + 4 * n_s * RET_HEADS * RET_DK * RET_DV * 4
           + 16 * 1024 * 1024)
    stream3 = lambda i: (i, 0, 0)
    return pl.pallas_call(
        functools.partial(_mixer_sample_body, n_s=n_s),
        out_shape=(
            jax.ShapeDtypeStruct((nb, CHUNK, D_MODEL), BF16),
            jax.ShapeDtypeStruct((nb, WINDOW, KV_COLS), F32),
            jax.ShapeDtypeStruct((nb, WINDOW, KV_COLS), F32),
            jax.ShapeDtypeStruct((nb, RET_HEADS, RET_DK, RET_DV), F32),
        ),
        grid=(nb // n_s,),
        in_specs=[
            pl.BlockSpec((n_s, CHUNK, IN_WIDTH), stream3),
            pl.BlockSpec((n_s, WINDOW, KV_COLS), stream3),
            pl.BlockSpec((n_s, WINDOW, KV_COLS), stream3),
            pl.BlockSpec((n_s, RET_HEADS, RET_DK, RET_DV), lambda i: (i, 0, 0, 0)),
            _const_spec(sink_col.shape),
            _const_spec(gq.shape),
            _const_spec(gk.shape),
            _const_spec(gret.shape),
            _const_spec(lg.shape),
            _const_spec(rope.shape),
        ],
        out_specs=(
            pl.BlockSpec((n_s, CHUNK, D_MODEL), stream3),
            pl.BlockSpec((n_s, WINDOW, KV_COLS), stream3),
            pl.BlockSpec((n_s, WINDOW, KV_COLS), stream3),
            pl.BlockSpec((n_s, RET_HEADS, RET_DK, RET_DV), lambda i: (i, 0, 0, 0)),
        ),
        scratch_shapes=[pltpu.VMEM((n_s * N_KV_HEADS, WINDOW + CHUNK, HEAD_DIM), F32),
                        pltpu.VMEM((n_s * N_KV_HEADS, WINDOW + CHUNK, HEAD_DIM), F32)],
        compiler_params=_params(("arbitrary",), est),
        name="mixer_sample",
    )(u.reshape(nb, CHUNK, IN_WIDTH), cache_k, cache_v, state, sink_col, gq, gk, gret, lg, rope)


def _mixer_tables(sinks, gret, n_s):
    sink_col = jnp.repeat(sinks.astype(F32), CHUNK).reshape(N_KV_HEADS, GROUP * CHUNK, 1)
    sink_col = jnp.tile(sink_col, (n_s, 1, 1))
    gret3 = jnp.tile(gret.astype(F32).reshape(RET_HEADS, 1, RET_DV), (n_s, 1, 1))
    lg = jnp.tile(jnp.asarray(np.array(LOG_GAMMA, np.float32).reshape(RET_HEADS, 1, 1)),
                  (n_s, 1, 1))
    return sink_col, gret3, lg


def _rope_table():
    half = RET_DK // 2
    inv_freq = (np.float32(ROPE_BASE) ** (-np.arange(half, dtype=np.float32) / np.float32(half)))
    inv_freq = inv_freq.astype(np.float32)
    sign = np.concatenate([-np.ones(half, np.float32), np.ones(half, np.float32)])
    return jnp.asarray(np.stack([np.concatenate([inv_freq, inv_freq]), sign]))


def kernel(x_prompt, x_sample, cache_attn_k, cache_attn_v, state_ret, p_prompt, p_sample, g_ffn1, w_ffn1_gate, w_ffn1_up, w_ffn1_down, g_mix, w_in, g_q, g_k, attn_sinks, g_ret, w_out, g_ffn2, w_ffn2_gate, w_ffn2_up, w_ffn2_down, g_ple, w_ple_gate, w_ple_proj):
    depth = g_ffn1.shape[0]
    batch, seq, d = x_prompt.shape
    nb, dec_len, _ = x_sample.shape
    assert d == D_MODEL and dec_len == CHUNK and seq % CHUNK == 0
    assert w_in.shape[2] == IN_WIDTH and nb % SAMPLE_STREAMS == 0
    rope = _rope_table()

    hp = x_prompt.reshape(batch * seq, d)
    hs = x_sample.reshape(nb * dec_len, d)
    kp_l, vp_l, sp_l, ks_l, vs_l, ss_l = [], [], [], [], [], []
    for i in range(depth):
        wg1 = w_ffn1_gate[i].astype(BF16)
        wu1 = w_ffn1_up[i].astype(BF16)
        wd1 = w_ffn1_down[i].astype(BF16)
        gf1 = g_ffn1[i].reshape(1, d)
        gf2 = g_ffn2[i].reshape(1, d)
        gmx = g_mix[i].reshape(1, d)
        gpl = g_ple[i].reshape(1, d)
        gq = g_q[i].reshape(1, HEAD_DIM)
        gk = g_k[i].reshape(1, HEAD_DIM)
        sinks = attn_sinks[i]
        gret = g_ret[i]

        later = ((w_ffn2_gate[i], FFN_TF), (w_ffn2_up[i], FFN_TF), (w_ffn2_down[i], d),
                 (w_in[i], PROJ_TN), (w_out[i], MERGE_TN), (w_ple_gate[i], PLE_TN),
                 (w_ple_proj[i], PLE_TN))
        hp, zp, (wg2, wu2, wd2, win, wout, wpg, wpp) = _ffn(hp, gf1, wg1, wu1, wd1, gmx, later)
        wd2 = wd2.reshape(wd2.shape[1], d)
        hs, zs, _ = _ffn(hs, gf1, wg1, wu1, wd1, gmx)

        up = _matmul(zp, win, PROJ_TM, name="proj")
        us = _matmul(zs, win, PROJ_TM, name="proj")
        sink_p, gret_p, lg_p = _mixer_tables(sinks, gret, batch)
        sink_s, gret_s, lg_s = _mixer_tables(sinks, gret, SAMPLE_STREAMS)
        mixp, kp, vp, sp = _mixer_prompt(up, sink_p, gq, gk, gret_p, lg_p, rope, batch, seq)
        mixs, ks, vs, ss = _mixer_sample(
            us, cache_attn_k[i].reshape(nb, WINDOW, KV_COLS),
            cache_attn_v[i].reshape(nb, WINDOW, KV_COLS), state_ret[i],
            sink_s, gq, gk, gret_s, lg_s, rope, SAMPLE_STREAMS)
        mixp = mixp.reshape(batch * seq, d)
        mixs = mixs.reshape(nb * dec_len, d)
        hp = _matmul(mixp, wout, MERGE_TM, h=hp, name="merge")
        hs = _matmul(mixs, wout, MERGE_TM, h=hs, name="merge")
        kp_l.append(kp.reshape(batch, WINDOW, N_KV_HEADS, HEAD_DIM))
        vp_l.append(vp.reshape(batch, WINDOW, N_KV_HEADS, HEAD_DIM))
        sp_l.append(sp)
        ks_l.append(ks.reshape(nb, WINDOW, N_KV_HEADS, HEAD_DIM))
        vs_l.append(vs.reshape(nb, WINDOW, N_KV_HEADS, HEAD_DIM))
        ss_l.append(ss)

        hp, zp, _ = _ffn(hp, gf2, wg2, wu2, wd2, gpl)
        hs, zs, _ = _ffn(hs, gf2, wg2, wu2, wd2, gpl)

        hp = _matmul(zp, wpg, PLE_TM, h=hp, p=p_prompt[i].reshape(batch * seq, -1),
                     wp=wpp, name="ple")
        hs = _matmul(zs, wpg, PLE_TM, h=hs, p=p_sample[i].reshape(nb * dec_len, -1),
                     wp=wpp, name="ple")

    return (hp.reshape(batch, seq, d), hs.reshape(nb, dec_len, d),
            jnp.stack(kp_l), jnp.stack(vp_l), jnp.stack(sp_l),
            jnp.stack(ks_l), jnp.stack(vs_l), jnp.stack(ss_l))
```

```python
import functools
import math

import numpy as np
import jax
import jax.numpy as jnp
from jax import lax
from jax.experimental import pallas as pl
from jax.experimental.pallas import tpu as pltpu

F32 = jnp.float32
BF16 = jnp.bfloat16

D_MODEL = 4096
CHUNK = 64
HEAD_DIM = 64
N_HEADS = 32
N_KV_HEADS = 4
GROUP = N_HEADS // N_KV_HEADS
WINDOW = 128
RET_HEADS = 8
RET_DK = 128
RET_DV = 256
ATTN_WIDTH = N_HEADS * HEAD_DIM
RET_WIDTH = RET_HEADS * RET_DV
PAST_LEN = 4096
ROPE_BASE = 10000.0
EPS = 1e-6

Q_OFF = 0
K_OFF = Q_OFF + N_HEADS * HEAD_DIM
V_OFF = K_OFF + N_KV_HEADS * HEAD_DIM
QR_OFF = V_OFF + N_KV_HEADS * HEAD_DIM
KR_OFF = QR_OFF + RET_HEADS * RET_DK
VR_OFF = KR_OFF + RET_HEADS * RET_DK
GR_OFF = VR_OFF + RET_WIDTH
IN_WIDTH = GR_OFF + RET_WIDTH
KV_COLS = N_KV_HEADS * HEAD_DIM

LOG2_E = math.log2(math.e)
LOG_GAMMA = tuple(math.log(1.0 - 2.0 ** (-5.0 - h)) for h in range(RET_HEADS))

V7X_VMEM_BYTES = 64 * 1024 * 1024
VMEM_CAP_BYTES = 60000 * 1024

FFN_TM = 1024
FFN_TF = 256
PROJ_TM, PROJ_TN = 2048, 512
MERGE_TM, MERGE_TN = 1024, 1024
PLE_TM, PLE_TN = 1024, 512
NORM_ROWS = 16
SAMPLE_STREAMS = 2
ROW_LOOP_UNROLL = 8
X_COPY_CHUNKS = 4


def _vmem_limit(estimate_bytes):
    return int(min(VMEM_CAP_BYTES, max(32 * 1024 * 1024, estimate_bytes * 5 // 4)))


def _params(semantics, vmem_estimate):
    return pltpu.CompilerParams(dimension_semantics=semantics,
                                vmem_limit_bytes=_vmem_limit(vmem_estimate))


def _row_loop(n_rows, body):
    def step(r, carry):
        body(pl.ds(pl.multiple_of(r * NORM_ROWS, NORM_ROWS), NORM_ROWS))
        return carry

    lax.fori_loop(0, n_rows // NORM_ROWS, step, 0, unroll=ROW_LOOP_UNROLL)


def _rms_rows(x, g):
    ms = jnp.mean(x * x, axis=-1, keepdims=True)
    return x * lax.rsqrt(ms + EPS) * g


def _cast_plan(shape, bc, n_steps):
    r, c = shape
    nc = c // bc
    fits = [nr for nr in range(1, r + 1)
            if r % nr == 0 and (r // nr) % 16 == 0 and nr * nc <= n_steps]
    if not fits:
        return None
    nr = max(fits)
    return r // nr, bc, nr, nc


def _ffn_body(*refs, n_i, n_f, tm, n_side):
    x_hbm, g_ref, gn_ref, wg_ref, wu_ref, wd_ref = refs[:6]
    side_in = refs[6:6 + n_side]
    o_hbm, zn_ref = refs[6 + n_side:8 + n_side]
    side_out = refs[8 + n_side:8 + 2 * n_side]
    o_ref, x_sems, o_sems = refs[8 + 2 * n_side:]
    i = pl.program_id(0)
    f = pl.program_id(1)
    rc = tm // X_COPY_CHUNKS

    def x_copy(c):
        return pltpu.make_async_copy(
            x_hbm.at[pl.ds(pl.multiple_of(i * tm, tm) + c * rc, rc), :],
            o_ref.at[pl.ds(c * rc, rc), :], x_sems.at[c])

    def o_copy(c, tile):
        return pltpu.make_async_copy(
            o_ref.at[pl.ds(c * rc, rc), :],
            o_hbm.at[pl.ds(pl.multiple_of(tile * tm, tm) + c * rc, rc), :], o_sems.at[c])

    @pl.when(f == 0)
    def _():
        for c in range(X_COPY_CHUNKS):
            @pl.when(i > 0)
            def _(c=c):
                o_copy(c, i - 1).wait()

            x_copy(c).start()
        g = g_ref[...]
        for c in range(X_COPY_CHUNKS):
            x_copy(c).wait()

            def init(rows, base=c * rc):
                rows = pl.ds(rows.start + base, NORM_ROWS)
                zn_ref[rows, :] = _rms_rows(o_ref[rows, :], g).astype(zn_ref.dtype)

            _row_loop(rc, init)

    for src, dst in zip(side_in, side_out):
        dst[...] = src[...].astype(dst.dtype)

    z = zn_ref[...]
    gate = jnp.dot(z, wg_ref[...], preferred_element_type=F32)
    up = jnp.dot(z, wu_ref[...], preferred_element_type=F32)
    a = (0.5 * gate * jax.nn.sigmoid(gate) * up).astype(BF16)
    o_ref[...] += jnp.dot(a, wd_ref[...], preferred_element_type=F32)

    @pl.when(f == n_f - 1)
    def _():
        for c in range(X_COPY_CHUNKS):
            o_copy(c, i).start()
        gn = gn_ref[...]

        def fin(rows):
            zn_ref[rows, :] = _rms_rows(o_ref[rows, :], gn).astype(zn_ref.dtype)

        _row_loop(tm, fin)

        @pl.when(i == n_i - 1)
        def _():
            for c in range(X_COPY_CHUNKS):
                o_copy(c, i).wait()


def _ffn(x, g, wg, wu, wd, g_next, side=()):
    m, d = x.shape
    tm = min(FFN_TM, m)
    tile_major = wg.ndim == 3
    tf = wg.shape[2] if tile_major else FFN_TF
    n_f = wg.shape[0] if tile_major else wg.shape[1] // tf
    n_i = m // tm
    if tile_major:
        w_spec = pl.BlockSpec((None, d, tf), lambda i, f: (f, 0, 0))
    else:
        w_spec = pl.BlockSpec((d, tf), lambda i, f: (0, f))
    assert tm % (X_COPY_CHUNKS * NORM_ROWS) == 0
    plans = [_cast_plan(w.shape, bc, n_i * n_f) for w, bc in side]
    assert all(p is not None for p in plans)

    def side_specs(plan):
        br, bc, nr, nc = plan

        def block(i, f):
            b = jnp.minimum(i * n_f + f, nr * nc - 1)
            return b // nc, b % nc

        def src_index(i, f):
            return block(i, f)

        def dst_index(i, f):
            rb, cb = block(i, f)
            return cb, rb, 0

        return pl.BlockSpec((br, bc), src_index), pl.BlockSpec((None, br, bc), dst_index)

    specs = [side_specs(p) for p in plans]
    est = (tm * d * 4 + 2 * tm * d * 2 + 2 * 3 * d * tf * 2
           + sum(2 * p[0] * p[1] * 6 for p in plans))
    outs = pl.pallas_call(
        functools.partial(_ffn_body, n_i=n_i, n_f=n_f, tm=tm, n_side=len(side)),
        out_shape=(jax.ShapeDtypeStruct((m, d), F32), jax.ShapeDtypeStruct((m, d), BF16))
        + tuple(jax.ShapeDtypeStruct((w.shape[1] // bc, w.shape[0], bc), BF16)
                for w, bc in side),
        grid=(n_i, n_f),
        in_specs=[
            pl.BlockSpec(memory_space=pl.ANY),
            pl.BlockSpec((1, d), lambda i, f: (0, 0)),
            pl.BlockSpec((1, d), lambda i, f: (0, 0)),
            w_spec,
            w_spec,
            pl.BlockSpec((tf, d), lambda i, f: (f, 0)),
        ] + [src for src, _ in specs],
        out_specs=(
            pl.BlockSpec(memory_space=pl.ANY),
            pl.BlockSpec((tm, d), lambda i, f: (i, 0)),
        ) + tuple(dst for _, dst in specs),
        scratch_shapes=[pltpu.VMEM((tm, d), F32),
                        pltpu.SemaphoreType.DMA((X_COPY_CHUNKS,)),
                        pltpu.SemaphoreType.DMA((X_COPY_CHUNKS,))],
        compiler_params=_params(("arbitrary", "arbitrary"), est),
        name="ffn",
    )(x, g, g_next, wg, wu, wd, *[w for w, _ in side])
    return outs[0], outs[1], outs[2:]


def _matmul_body(*refs, has_h, has_p):
    z_ref, w_ref = refs[:2]
    o_ref = refs[-1]
    acc = jnp.dot(z_ref[...], w_ref[...], preferred_element_type=F32)
    if has_p:
        h_ref, p_ref, wp_ref = refs[2:5]
        proj = jnp.dot(p_ref[...].astype(BF16), wp_ref[...], preferred_element_type=F32)
        o_ref[...] = h_ref[...] + jax.nn.sigmoid(acc) * proj
    elif has_h:
        o_ref[...] = refs[2][...] + acc
    else:
        o_ref[...] = acc


def _matmul(z, w, tm, h=None, p=None, wp=None, name="matmul"):
    m, k = z.shape
    n_j, _, tn = w.shape
    n = n_j * tn
    tm = min(tm, m)
    operands = [z, w]
    in_specs = [pl.BlockSpec((tm, k), lambda i, j: (i, 0)),
                pl.BlockSpec((None, k, tn), lambda i, j: (j, 0, 0))]
    est = 2 * tm * k * 2 + 2 * k * tn * 2 + 2 * tm * tn * 4
    if h is not None:
        operands.append(h)
        in_specs.append(pl.BlockSpec((tm, tn), lambda i, j: (i, j)))
        est += 2 * tm * tn * 4
    if p is not None:
        pd = p.shape[1]
        operands += [p, wp]
        in_specs += [pl.BlockSpec((tm, pd), lambda i, j: (i, 0)),
                     pl.BlockSpec((None, pd, tn), lambda i, j: (j, 0, 0))]
        est += 2 * tm * pd * 4 + 2 * pd * tn * 2
    return pl.pallas_call(
        functools.partial(_matmul_body, has_h=h is not None, has_p=p is not None),
        out_shape=jax.ShapeDtypeStruct((m, n), F32),
        grid=(m // tm, n_j),
        in_specs=in_specs,
        out_specs=pl.BlockSpec((tm, tn), lambda i, j: (i, j)),
        compiler_params=_params(("arbitrary", "arbitrary"), est),
        name=name,
    )(*operands)


def _rms_last(x, g):
    ms = jnp.mean(x * x, axis=-1, keepdims=True)
    return x * lax.rsqrt(ms + EPS) * g


def _stack_cols(u_ref, n_s, off, width, count):
    return jnp.concatenate(
        [u_ref[s, :, off + j * width:off + (j + 1) * width]
         for s in range(n_s) for j in range(count)], axis=0)


def _bdot_nt(a, b):
    return lax.dot_general(a, b, (((2,), (2,)), ((0,), (0,))), preferred_element_type=F32)


def _bdot_nn(a, b):
    return lax.dot_general(a, b, (((2,), (1,)), ((0,), (0,))), preferred_element_type=F32)


def _new_kv(u_ref, gk, n_s, kwin_ref, vwin_ref):
    nb = n_s * N_KV_HEADS
    knew = _rms_last(_stack_cols(u_ref, n_s, K_OFF, HEAD_DIM, N_KV_HEADS), gk)
    vnew = _stack_cols(u_ref, n_s, V_OFF, HEAD_DIM, N_KV_HEADS)
    kwin_ref[:, WINDOW:WINDOW + CHUNK, :] = knew.reshape(nb, CHUNK, HEAD_DIM)
    vwin_ref[:, WINDOW:WINDOW + CHUNK, :] = vnew.reshape(nb, CHUNK, HEAD_DIM)


def _attention_chunk(u_ref, kwin_ref, vwin_ref, sink_ref, gq, valid, mix_ref, n_s):
    nb = n_s * N_KV_HEADS
    q = _rms_last(_stack_cols(u_ref, n_s, Q_OFF, HEAD_DIM, N_HEADS), gq)
    q = q.astype(BF16).reshape(nb, GROUP * CHUNK, HEAD_DIM)
    s = _bdot_nt(q, kwin_ref[...].astype(BF16)) * (HEAD_DIM ** -0.5 * LOG2_E)
    if valid is not None:
        s = jnp.where(valid, s, -jnp.inf)
    sink = sink_ref[...] * LOG2_E
    m = jnp.maximum(jnp.max(s, axis=-1, keepdims=True), sink)
    e = jnp.exp2(s - m).astype(BF16)
    ones = jnp.ones((nb, WINDOW + CHUNK, HEAD_DIM), BF16)
    denom = _bdot_nn(e, ones) + jnp.exp2(sink - m)
    o = _bdot_nn(e, vwin_ref[...].astype(BF16)) / denom
    for st in range(n_s):
        pieces = [o[st * N_KV_HEADS + k, g * CHUNK:(g + 1) * CHUNK, :]
                  for k in range(N_KV_HEADS) for g in range(GROUP)]
        mix_ref[st, :, 0:ATTN_WIDTH] = jnp.concatenate(pieces, axis=-1).astype(mix_ref.dtype)


def _retention_chunk(u_ref, rope_ref, lg_ref, gret_ref, pos0, state, mix_ref, n_s):
    nb = n_s * RET_HEADS
    row = lax.broadcasted_iota(jnp.int32, (CHUNK, 1), 0)
    pos = (pos0 + row).astype(F32)
    ang = pos * rope_ref[0:1, :]
    cosv = jnp.cos(ang)[None]
    sinv = (jnp.sin(ang) * rope_ref[1:2, :])[None]
    lg = lg_ref[...]
    ri = lax.broadcasted_iota(jnp.int32, (CHUNK, CHUNK), 0)
    ci = lax.broadcasted_iota(jnp.int32, (CHUNK, CHUNK), 1)
    dist = jnp.abs(ri - ci).astype(F32)[None]
    rowf = row.astype(F32)[None]

    def rope(x):
        rolled = pltpu.roll(x, RET_DK // 2, 1)
        return (x.reshape(nb, CHUNK, RET_DK) * cosv
                + rolled.reshape(nb, CHUNK, RET_DK) * sinv)

    q = rope(_stack_cols(u_ref, n_s, QR_OFF, RET_DK, RET_HEADS))
    kk = rope(_stack_cols(u_ref, n_s, KR_OFF, RET_DK, RET_HEADS)) * (RET_DK ** -0.5)
    v = _stack_cols(u_ref, n_s, VR_OFF, RET_DV, RET_HEADS).astype(BF16)
    v = v.reshape(nb, CHUNK, RET_DV)
    qb = q.astype(BF16)
    sc = _bdot_nt(qb, kk.astype(BF16)) * jnp.exp(lg * dist)
    out = _bdot_nn(sc.astype(BF16), v)
    out = out + _bdot_nn(qb, state.astype(BF16)) * jnp.exp(lg * (rowf + 1.0))
    kd = kk * jnp.exp(lg * ((CHUNK - 1.0) - rowf))
    upd = _bdot_nn(jnp.swapaxes(kd, 1, 2).astype(BF16), v)
    new_state = jnp.exp(lg * float(CHUNK)) * state + upd
    mu = jnp.mean(out, axis=-1, keepdims=True)
    cen = out - mu
    var = jnp.mean(cen * cen, axis=-1, keepdims=True)
    normed = cen * lax.rsqrt(var + EPS) * gret_ref[...]
    for st in range(n_s):
        gate = u_ref[st, :, GR_OFF:GR_OFF + RET_WIDTH]
        flat = jnp.concatenate([normed[st * RET_HEADS + h] for h in range(RET_HEADS)], axis=-1)
        ret = flat * (gate * jax.nn.sigmoid(gate))
        mix_ref[st, :, ATTN_WIDTH:ATTN_WIDTH + RET_WIDTH] = ret.astype(mix_ref.dtype)
    return new_state


def _window_to_rows(win_ref, st):
    return jnp.concatenate([win_ref[st * N_KV_HEADS + k, CHUNK:WINDOW + CHUNK, :]
                            for k in range(N_KV_HEADS)], axis=-1)


def _mixer_prompt_body(u_ref, sink_ref, gq_ref, gk_ref, gret_ref, lg_ref, rope_ref,
                       mix_ref, ko_ref, vo_ref, st_ref, kwin_ref, vwin_ref, *, n_chunks, n_s):
    c = pl.program_id(0)
    nbk = n_s * N_KV_HEADS

    @pl.when(c == 0)
    def _():
        kwin_ref[:, 0:WINDOW, :] = jnp.zeros((nbk, WINDOW, HEAD_DIM), F32)
        vwin_ref[:, 0:WINDOW, :] = jnp.zeros((nbk, WINDOW, HEAD_DIM), F32)
        st_ref[...] = jnp.zeros(st_ref.shape, F32)

    _new_kv(u_ref, gk_ref[...], n_s, kwin_ref, vwin_ref)
    key_chunk = lax.broadcasted_iota(jnp.int32, (1, 1, WINDOW + CHUNK), 2) // CHUNK
    valid = (c + key_chunk) >= (WINDOW // CHUNK)
    _attention_chunk(u_ref, kwin_ref, vwin_ref, sink_ref, gq_ref[...], valid, mix_ref, n_s)

    state = st_ref[...].reshape(n_s * RET_HEADS, RET_DK, RET_DV)
    new_state = _retention_chunk(u_ref, rope_ref, lg_ref, gret_ref, c * CHUNK, state, mix_ref, n_s)
    st_ref[...] = new_state.reshape(st_ref.shape)

    @pl.when(c == n_chunks - 1)
    def _():
        for st in range(n_s):
            ko_ref[st] = _window_to_rows(kwin_ref, st)
            vo_ref[st] = _window_to_rows(vwin_ref, st)

    kshift = kwin_ref[:, CHUNK:WINDOW + CHUNK, :]
    vshift = vwin_ref[:, CHUNK:WINDOW + CHUNK, :]
    kwin_ref[:, 0:WINDOW, :] = kshift
    vwin_ref[:, 0:WINDOW, :] = vshift


def _const_spec(shape):
    return pl.BlockSpec(shape, lambda *_: (0,) * len(shape))


def _mixer_prompt(u, sink_col, gq, gk, gret, lg, rope, batch, seq):
    n_chunks = seq // CHUNK
    n_s = batch
    est = (2 * n_s * CHUNK * IN_WIDTH * 4 + 2 * n_s * RET_HEADS * RET_DK * RET_DV * 4
           + 16 * 1024 * 1024)
    return pl.pallas_call(
        functools.partial(_mixer_prompt_body, n_chunks=n_chunks, n_s=n_s),
        out_shape=(
            jax.ShapeDtypeStruct((batch, seq, D_MODEL), BF16),
            jax.ShapeDtypeStruct((batch, WINDOW, KV_COLS), F32),
            jax.ShapeDtypeStruct((batch, WINDOW, KV_COLS), F32),
            jax.ShapeDtypeStruct((batch, RET_HEADS, RET_DK, RET_DV), F32),
        ),
        grid=(n_chunks,),
        in_specs=[
            pl.BlockSpec((n_s, CHUNK, IN_WIDTH), lambda c: (0, c, 0)),
            _const_spec(sink_col.shape),
            _const_spec(gq.shape),
            _const_spec(gk.shape),
            _const_spec(gret.shape),
            _const_spec(lg.shape),
            _const_spec(rope.shape),
        ],
        out_specs=(
            pl.BlockSpec((n_s, CHUNK, D_MODEL), lambda c: (0, c, 0)),
            _const_spec((batch, WINDOW, KV_COLS)),
            _const_spec((batch, WINDOW, KV_COLS)),
            _const_spec((batch, RET_HEADS, RET_DK, RET_DV)),
        ),
        scratch_shapes=[pltpu.VMEM((n_s * N_KV_HEADS, WINDOW + CHUNK, HEAD_DIM), F32),
                        pltpu.VMEM((n_s * N_KV_HEADS, WINDOW + CHUNK, HEAD_DIM), F32)],
        compiler_params=_params(("arbitrary",), est),
        name="mixer_prompt",
    )(u.reshape(batch, seq, IN_WIDTH), sink_col, gq, gk, gret, lg, rope)


def _mixer_sample_body(u_ref, ck_ref, cv_ref, sin_ref, sink_ref, gq_ref, gk_ref, gret_ref,
                       lg_ref, rope_ref, mix_ref, ko_ref, vo_ref, st_ref, kwin_ref, vwin_ref,
                       *, n_s):
    for st in range(n_s):
        for k in range(N_KV_HEADS):
            cols = slice(k * HEAD_DIM, (k + 1) * HEAD_DIM)
            kwin_ref[st * N_KV_HEADS + k, 0:WINDOW, :] = ck_ref[st, :, cols]
            vwin_ref[st * N_KV_HEADS + k, 0:WINDOW, :] = cv_ref[st, :, cols]
    _new_kv(u_ref, gk_ref[...], n_s, kwin_ref, vwin_ref)
    _attention_chunk(u_ref, kwin_ref, vwin_ref, sink_ref, gq_ref[...], None, mix_ref, n_s)
    state = sin_ref[...].reshape(n_s * RET_HEADS, RET_DK, RET_DV)
    new_state = _retention_chunk(u_ref, rope_ref, lg_ref, gret_ref, PAST_LEN, state, mix_ref, n_s)
    st_ref[...] = new_state.reshape(st_ref.shape)
    for st in range(n_s):
        ko_ref[st] = _window_to_rows(kwin_ref, st)
        vo_ref[st] = _window_to_rows(vwin_ref, st)


def _mixer_sample(u, cache_k, cache_v, state, sink_col, gq, gk, gret, lg, rope, n_s):
    nb = cache_k.shape[0]
    est = (2 * n_s * CHUNK * IN_WIDTH * 4 + 4 * n_s * RET_HEADS * RET_DK * RET_DV * 4
           + 16 * 1024 * 1024)
    stream3 = lambda i: (i, 0, 0)
    return pl.pallas_call(
        functools.partial(_mixer_sample_body, n_s=n_s),
        out_shape=(
            jax.ShapeDtypeStruct((nb, CHUNK, D_MODEL), BF16),
            jax.ShapeDtypeStruct((nb, WINDOW, KV_COLS), F32),
            jax.ShapeDtypeStruct((nb, WINDOW, KV_COLS), F32),
            jax.ShapeDtypeStruct((nb, RET_HEADS, RET_DK, RET_DV), F32),
        ),
        grid=(nb // n_s,),
        in_specs=[
            pl.BlockSpec((n_s, CHUNK, IN_WIDTH), stream3),
            pl.BlockSpec((n_s, WINDOW, KV_COLS), stream3),
            pl.BlockSpec((n_s, WINDOW, KV_COLS), stream3),
            pl.BlockSpec((n_s, RET_HEADS, RET_DK, RET_DV), lambda i: (i, 0, 0, 0)),
            _const_spec(sink_col.shape),
            _const_spec(gq.shape),
            _const_spec(gk.shape),
            _const_spec(gret.shape),
            _const_spec(lg.shape),
            _const_spec(rope.shape),
        ],
        out_specs=(
            pl.BlockSpec((n_s, CHUNK, D_MODEL), stream3),
            pl.BlockSpec((n_s, WINDOW, KV_COLS), stream3),
            pl.BlockSpec((n_s, WINDOW, KV_COLS), stream3),
            pl.BlockSpec((n_s, RET_HEADS, RET_DK, RET_DV), lambda i: (i, 0, 0, 0)),
        ),
        scratch_shapes=[pltpu.VMEM((n_s * N_KV_HEADS, WINDOW + CHUNK, HEAD_DIM), F32),
                        pltpu.VMEM((n_s * N_KV_HEADS, WINDOW + CHUNK, HEAD_DIM), F32)],
        compiler_params=_params(("arbitrary",), est),
        name="mixer_sample",
    )(u.reshape(nb, CHUNK, IN_WIDTH), cache_k, cache_v, state, sink_col, gq, gk, gret, lg, rope)


def _mixer_tables(sinks, gret, n_s):
    sink_col = jnp.repeat(sinks.astype(F32), CHUNK).reshape(N_KV_HEADS, GROUP * CHUNK, 1)
    sink_col = jnp.tile(sink_col, (n_s, 1, 1))
    gret3 = jnp.tile(gret.astype(F32).reshape(RET_HEADS, 1, RET_DV), (n_s, 1, 1))
    lg = jnp.tile(jnp.asarray(np.array(LOG_GAMMA, np.float32).reshape(RET_HEADS, 1, 1)),
                  (n_s, 1, 1))
    return sink_col, gret3, lg


def _rope_table():
    half = RET_DK // 2
    inv_freq = (np.float32(ROPE_BASE) ** (-np.arange(half, dtype=np.float32) / np.float32(half)))
    inv_freq = inv_freq.astype(np.float32)
    sign = np.concatenate([-np.ones(half, np.float32), np.ones(half, np.float32)])
    return jnp.asarray(np.stack([np.concatenate([inv_freq, inv_freq]), sign]))


def kernel(x_prompt, x_sample, cache_attn_k, cache_attn_v, state_ret, p_prompt, p_sample, g_ffn1, w_ffn1_gate, w_ffn1_up, w_ffn1_down, g_mix, w_in, g_q, g_k, attn_sinks, g_ret, w_out, g_ffn2, w_ffn2_gate, w_ffn2_up, w_ffn2_down, g_ple, w_ple_gate, w_ple_proj):
    depth = g_ffn1.shape[0]
    batch, seq, d = x_prompt.shape
    nb, dec_len, _ = x_sample.shape
    assert d == D_MODEL and dec_len == CHUNK and seq % CHUNK == 0
    assert w_in.shape[2] == IN_WIDTH and nb % SAMPLE_STREAMS == 0
    rope = _rope_table()

    hp = x_prompt.reshape(batch * seq, d)
    hs = x_sample.reshape(nb * dec_len, d)
    kp_l, vp_l, sp_l, ks_l, vs_l, ss_l = [], [], [], [], [], []
    for i in range(depth):
        wg1 = w_ffn1_gate[i].astype(BF16)
        wu1 = w_ffn1_up[i].astype(BF16)
        wd1 = w_ffn1_down[i].astype(BF16)
        gf1 = g_ffn1[i].reshape(1, d)
        gf2 = g_ffn2[i].reshape(1, d)
        gmx = g_mix[i].reshape(1, d)
        gpl = g_ple[i].reshape(1, d)
        gq = g_q[i].reshape(1, HEAD_DIM)
        gk = g_k[i].reshape(1, HEAD_DIM)
        sinks = attn_sinks[i]
        gret = g_ret[i]

        later = ((w_ffn2_gate[i], FFN_TF), (w_ffn2_up[i], FFN_TF), (w_ffn2_down[i], d),
                 (w_in[i], PROJ_TN), (w_out[i], MERGE_TN), (w_ple_gate[i], PLE_TN),
                 (w_ple_proj[i], PLE_TN))
        hp, zp, (wg2, wu2, wd2, win, wout, wpg, wpp) = _ffn(hp, gf1, wg1, wu1, wd1, gmx, later)
        wd2 = wd2.reshape(wd2.shape[1], d)
        hs, zs, _ = _ffn(hs, gf1, wg1, wu1, wd1, gmx)

        up = _matmul(zp, win, PROJ_TM, name="proj")
        us = _matmul(zs, win, PROJ_TM, name="proj")
        sink_p, gret_p, lg_p = _mixer_tables(sinks, gret, batch)
        sink_s, gret_s, lg_s = _mixer_tables(sinks, gret, SAMPLE_STREAMS)
        mixp, kp, vp, sp = _mixer_prompt(up, sink_p, gq, gk, gret_p, lg_p, rope, batch, seq)
        mixs, ks, vs, ss = _mixer_sample(
            us, cache_attn_k[i].reshape(nb, WINDOW, KV_COLS),
            cache_attn_v[i].reshape(nb, WINDOW, KV_COLS), state_ret[i],
            sink_s, gq, gk, gret_s, lg_s, rope, SAMPLE_STREAMS)
        mixp = mixp.reshape(batch * seq, d)
        mixs = mixs.reshape(nb * dec_len, d)
        hp = _matmul(mixp, wout, MERGE_TM, h=hp, name="merge")
        hs = _matmul(mixs, wout, MERGE_TM, h=hs, name="merge")
        kp_l.append(kp.reshape(batch, WINDOW, N_KV_HEADS, HEAD_DIM))
        vp_l.append(vp.reshape(batch, WINDOW, N_KV_HEADS, HEAD_DIM))
        sp_l.append(sp)
        ks_l.append(ks.reshape(nb, WINDOW, N_KV_HEADS, HEAD_DIM))
        vs_l.append(vs.reshape(nb, WINDOW, N_KV_HEADS, HEAD_DIM))
        ss_l.append(ss)

        hp, zp, _ = _ffn(hp, gf2, wg2, wu2, wd2, gpl)
        hs, zs, _ = _ffn(hs, gf2, wg2, wu2, wd2, gpl)

        hp = _matmul(zp, wpg, PLE_TM, h=hp, p=p_prompt[i].reshape(batch * seq, -1),
                     wp=wpp, name="ple")
        hs = _matmul(zs, wpg, PLE_TM, h=hs, p=p_sample[i].reshape(nb * dec_len, -1),
                     wp=wpp, name="ple")

    return (hp.reshape(batch, seq, d), hs.reshape(nb, dec_len, d),
            jnp.stack(kp_l), jnp.stack(vp_l), jnp.stack(sp_l),
            jnp.stack(ks_l), jnp.stack(vs_l), jnp.stack(ss_l))
```

```python
import functools
import math

import numpy as np
import jax
import jax.numpy as jnp
from jax import lax
from jax.experimental import pallas as pl
from jax.experimental.pallas import tpu as pltpu

F32 = jnp.float32
BF16 = jnp.bfloat16

D_MODEL = 4096
CHUNK = 64
HEAD_DIM = 64
N_HEADS = 32
N_KV_HEADS = 4
GROUP = N_HEADS // N_KV_HEADS
WINDOW = 128
RET_HEADS = 8
RET_DK = 128
RET_DV = 256
ATTN_WIDTH = N_HEADS * HEAD_DIM
RET_WIDTH = RET_HEADS * RET_DV
PAST_LEN = 4096
ROPE_BASE = 10000.0
EPS = 1e-6

Q_OFF = 0
K_OFF = Q_OFF + N_HEADS * HEAD_DIM
V_OFF = K_OFF + N_KV_HEADS * HEAD_DIM
QR_OFF = V_OFF + N_KV_HEADS * HEAD_DIM
KR_OFF = QR_OFF + RET_HEADS * RET_DK
VR_OFF = KR_OFF + RET_HEADS * RET_DK
GR_OFF = VR_OFF + RET_WIDTH
IN_WIDTH = GR_OFF + RET_WIDTH
KV_COLS = N_KV_HEADS * HEAD_DIM

LOG2_E = math.log2(math.e)
LOG_GAMMA = tuple(math.log(1.0 - 2.0 ** (-5.0 - h)) for h in range(RET_HEADS))

VMEM_CAP_BYTES = 60000 * 1024

FFN_TM = 1024
FFN_TF = 256
PROJ_TM, PROJ_TN = 2048, 512
MERGE_TM, MERGE_TN = 1024, 1024
PLE_TM, PLE_TN = 1024, 512
NORM_ROWS = 16
SAMPLE_STREAMS = 2
ROW_LOOP_UNROLL = 8
X_COPY_CHUNKS = 4


def _vmem_limit(estimate_bytes):
    return int(min(VMEM_CAP_BYTES, max(32 * 1024 * 1024, estimate_bytes * 5 // 4)))


def _params(semantics, vmem_estimate):
    return pltpu.CompilerParams(dimension_semantics=semantics,
                                vmem_limit_bytes=_vmem_limit(vmem_estimate))


def _row_loop(n_rows, body):
    def step(r, carry):
        body(pl.ds(pl.multiple_of(r * NORM_ROWS, NORM_ROWS), NORM_ROWS))
        return carry

    lax.fori_loop(0, n_rows // NORM_ROWS, step, 0, unroll=ROW_LOOP_UNROLL)


def _rms_rows(x, g):
    ms = jnp.mean(x * x, axis=-1, keepdims=True)
    return x * lax.rsqrt(ms + EPS) * g


def _cast_plan(shape, bc, n_steps):
    r, c = shape
    nc = c // bc
    fits = [nr for nr in range(1, r + 1)
            if r % nr == 0 and (r // nr) % 16 == 0 and nr * nc <= n_steps]
    if not fits:
        return None
    nr = max(fits)
    return r // nr, bc, nr, nc


def _ffn_body(*refs, n_i, n_f, tm, n_side):
    x_hbm, g_ref, gn_ref, wg_ref, wu_ref, wd_ref = refs[:6]
    side_in = refs[6:6 + n_side]
    o_hbm, zn_ref = refs[6 + n_side:8 + n_side]
    side_out = refs[8 + n_side:8 + 2 * n_side]
    o_ref, x_sems, o_sems = refs[8 + 2 * n_side:]
    i = pl.program_id(0)
    f = pl.program_id(1)
    rc = tm // X_COPY_CHUNKS

    def x_copy(c):
        return pltpu.make_async_copy(
            x_hbm.at[pl.ds(pl.multiple_of(i * tm, tm) + c * rc, rc), :],
            o_ref.at[pl.ds(c * rc, rc), :], x_sems.at[c])

    def o_copy(c, tile):
        return pltpu.make_async_copy(
            o_ref.at[pl.ds(c * rc, rc), :],
            o_hbm.at[pl.ds(pl.multiple_of(tile * tm, tm) + c * rc, rc), :], o_sems.at[c])

    @pl.when(f == 0)
    def _():
        for c in range(X_COPY_CHUNKS):
            @pl.when(i > 0)
            def _(c=c):
                o_copy(c, i - 1).wait()

            x_copy(c).start()
        g = g_ref[...]
        for c in range(X_COPY_CHUNKS):
            x_copy(c).wait()

            def init(rows, base=c * rc):
                rows = pl.ds(rows.start + base, NORM_ROWS)
                zn_ref[rows, :] = _rms_rows(o_ref[rows, :], g).astype(zn_ref.dtype)

            _row_loop(rc, init)

    for src, dst in zip(side_in, side_out):
        dst[...] = src[...].astype(dst.dtype)

    z = zn_ref[...]
    gate = jnp.dot(z, wg_ref[...], preferred_element_type=F32)
    up = jnp.dot(z, wu_ref[...], preferred_element_type=F32)
    a = (0.5 * gate * jax.nn.sigmoid(gate) * up).astype(BF16)
    o_ref[...] += jnp.dot(a, wd_ref[...], preferred_element_type=F32)

    @pl.when(f == n_f - 1)
    def _():
        for c in range(X_COPY_CHUNKS):
            o_copy(c, i).start()
        gn = gn_ref[...]

        def fin(rows):
            zn_ref[rows, :] = _rms_rows(o_ref[rows, :], gn).astype(zn_ref.dtype)

        _row_loop(tm, fin)

        @pl.when(i == n_i - 1)
        def _():
            for c in range(X_COPY_CHUNKS):
                o_copy(c, i).wait()


def _ffn(x, g, wg, wu, wd, g_next, side=()):
    m, d = x.shape
    tm = min(FFN_TM, m)
    tile_major = wg.ndim == 3
    tf = wg.shape[2] if tile_major else FFN_TF
    n_f = wg.shape[0] if tile_major else wg.shape[1] // tf
    n_i = m // tm
    if tile_major:
        w_spec = pl.BlockSpec((None, d, tf), lambda i, f: (f, 0, 0))
    else:
        w_spec = pl.BlockSpec((d, tf), lambda i, f: (0, f))
    assert tm % (X_COPY_CHUNKS * NORM_ROWS) == 0
    plans = [_cast_plan(w.shape, bc, n_i * n_f) for w, bc in side]
    assert all(p is not None for p in plans)

    def side_specs(plan):
        br, bc, nr, nc = plan

        def block(i, f):
            b = jnp.minimum(i * n_f + f, nr * nc - 1)
            return b // nc, b % nc

        def src_index(i, f):
            return block(i, f)

        def dst_index(i, f):
            rb, cb = block(i, f)
            return cb, rb, 0

        return pl.BlockSpec((br, bc), src_index), pl.BlockSpec((None, br, bc), dst_index)

    specs = [side_specs(p) for p in plans]
    est = (tm * d * 4 + 2 * tm * d * 2 + 2 * 3 * d * tf * 2
           + sum(2 * p[0] * p[1] * 6 for p in plans))
    outs = pl.pallas_call(
        functools.partial(_ffn_body, n_i=n_i, n_f=n_f, tm=tm, n_side=len(side)),
        out_shape=(jax.ShapeDtypeStruct((m, d), F32), jax.ShapeDtypeStruct((m, d), BF16))
        + tuple(jax.ShapeDtypeStruct((w.shape[1] // bc, w.shape[0], bc), BF16)
                for w, bc in side),
        grid=(n_i, n_f),
        in_specs=[
            pl.BlockSpec(memory_space=pl.ANY),
            pl.BlockSpec((1, d), lambda i, f: (0, 0)),
            pl.BlockSpec((1, d), lambda i, f: (0, 0)),
            w_spec,
            w_spec,
            pl.BlockSpec((tf, d), lambda i, f: (f, 0)),
        ] + [src for src, _ in specs],
        out_specs=(
            pl.BlockSpec(memory_space=pl.ANY),
            pl.BlockSpec((tm, d), lambda i, f: (i, 0)),
        ) + tuple(dst for _, dst in specs),
        scratch_shapes=[pltpu.VMEM((tm, d), F32),
                        pltpu.SemaphoreType.DMA((X_COPY_CHUNKS,)),
                        pltpu.SemaphoreType.DMA((X_COPY_CHUNKS,))],
        compiler_params=_params(("arbitrary", "arbitrary"), est),
        name="ffn",
    )(x, g, g_next, wg, wu, wd, *[w for w, _ in side])
    return outs[0], outs[1], outs[2:]


def _matmul_body(*refs, has_h, has_p):
    z_ref, w_ref = refs[:2]
    o_ref = refs[-1]
    acc = jnp.dot(z_ref[...], w_ref[...], preferred_element_type=F32)
    if has_p:
        h_ref, p_ref, wp_ref = refs[2:5]
        proj = jnp.dot(p_ref[...].astype(BF16), wp_ref[...], preferred_element_type=F32)
        o_ref[...] = h_ref[...] + jax.nn.sigmoid(acc) * proj
    elif has_h:
        o_ref[...] = refs[2][...] + acc
    else:
        o_ref[...] = acc


def _matmul(z, w, tm, h=None, p=None, wp=None, name="matmul"):
    m, k = z.shape
    n_j, _, tn = w.shape
    n = n_j * tn
    tm = min(tm, m)
    operands = [z, w]
    in_specs = [pl.BlockSpec((tm, k), lambda i, j: (i, 0)),
                pl.BlockSpec((None, k, tn), lambda i, j: (j, 0, 0))]
    est = 2 * tm * k * 2 + 2 * k * tn * 2 + 2 * tm * tn * 4
    if h is not None:
        operands.append(h)
        in_specs.append(pl.BlockSpec((tm, tn), lambda i, j: (i, j)))
        est += 2 * tm * tn * 4
    if p is not None:
        pd = p.shape[1]
        operands += [p, wp]
        in_specs += [pl.BlockSpec((tm, pd), lambda i, j: (i, 0)),
                     pl.BlockSpec((None, pd, tn), lambda i, j: (j, 0, 0))]
        est += 2 * tm * pd * 4 + 2 * pd * tn * 2
    return pl.pallas_call(
        functools.partial(_matmul_body, has_h=h is not None, has_p=p is not None),
        out_shape=jax.ShapeDtypeStruct((m, n), F32),
        grid=(m // tm, n_j),
        in_specs=in_specs,
        out_specs=pl.BlockSpec((tm, tn), lambda i, j: (i, j)),
        compiler_params=_params(("arbitrary", "arbitrary"), est),
        name=name,
    )(*operands)


def _stack_cols(u_ref, n_s, off, width, count):
    return jnp.concatenate(
        [u_ref[s, :, off + j * width:off + (j + 1) * width]
         for s in range(n_s) for j in range(count)], axis=0)


def _bdot_nt(a, b):
    return lax.dot_general(a, b, (((2,), (2,)), ((0,), (0,))), preferred_element_type=F32)


def _bdot_nn(a, b):
    return lax.dot_general(a, b, (((2,), (1,)), ((0,), (0,))), preferred_element_type=F32)


def _new_kv(u_ref, gk, n_s, kwin_ref, vwin_ref):
    nb = n_s * N_KV_HEADS
    knew = _rms_rows(_stack_cols(u_ref, n_s, K_OFF, HEAD_DIM, N_KV_HEADS), gk)
    vnew = _stack_cols(u_ref, n_s, V_OFF, HEAD_DIM, N_KV_HEADS)
    kwin_ref[:, WINDOW:WINDOW + CHUNK, :] = knew.reshape(nb, CHUNK, HEAD_DIM)
    vwin_ref[:, WINDOW:WINDOW + CHUNK, :] = vnew.reshape(nb, CHUNK, HEAD_DIM)


def _attention_chunk(u_ref, kwin_ref, vwin_ref, sink_ref, gq, valid, mix_ref, n_s):
    nb = n_s * N_KV_HEADS
    q = _rms_rows(_stack_cols(u_ref, n_s, Q_OFF, HEAD_DIM, N_HEADS), gq)
    q = q.astype(BF16).reshape(nb, GROUP * CHUNK, HEAD_DIM)
    s = _bdot_nt(q, kwin_ref[...].astype(BF16)) * (HEAD_DIM ** -0.5 * LOG2_E)
    if valid is not None:
        s = jnp.where(valid, s, -jnp.inf)
    sink = sink_ref[...] * LOG2_E
    m = jnp.maximum(jnp.max(s, axis=-1, keepdims=True), sink)
    e = jnp.exp2(s - m).astype(BF16)
    ones = jnp.ones((nb, WINDOW + CHUNK, HEAD_DIM), BF16)
    denom = _bdot_nn(e, ones) + jnp.exp2(sink - m)
    o = _bdot_nn(e, vwin_ref[...].astype(BF16)) / denom
    for st in range(n_s):
        pieces = [o[st * N_KV_HEADS + k, g * CHUNK:(g + 1) * CHUNK, :]
                  for k in range(N_KV_HEADS) for g in range(GROUP)]
        mix_ref[st, :, 0:ATTN_WIDTH] = jnp.concatenate(pieces, axis=-1).astype(mix_ref.dtype)


def _retention_chunk(u_ref, rope_ref, lg_ref, gret_ref, pos0, state, mix_ref, n_s):
    nb = n_s * RET_HEADS
    row = lax.broadcasted_iota(jnp.int32, (CHUNK, 1), 0)
    pos = (pos0 + row).astype(F32)
    ang = pos * rope_ref[0:1, :]
    cosv = jnp.cos(ang)[None]
    sinv = (jnp.sin(ang) * rope_ref[1:2, :])[None]
    lg = lg_ref[...]
    ri = lax.broadcasted_iota(jnp.int32, (CHUNK, CHUNK), 0)
    ci = lax.broadcasted_iota(jnp.int32, (CHUNK, CHUNK), 1)
    dist = jnp.abs(ri - ci).astype(F32)[None]
    rowf = row.astype(F32)[None]

    def rope(x):
        rolled = pltpu.roll(x, RET_DK // 2, 1)
        return (x.reshape(nb, CHUNK, RET_DK) * cosv
                + rolled.reshape(nb, CHUNK, RET_DK) * sinv)

    q = rope(_stack_cols(u_ref, n_s, QR_OFF, RET_DK, RET_HEADS))
    kk = rope(_stack_cols(u_ref, n_s, KR_OFF, RET_DK, RET_HEADS)) * (RET_DK ** -0.5)
    v = _stack_cols(u_ref, n_s, VR_OFF, RET_DV, RET_HEADS).astype(BF16)
    v = v.reshape(nb, CHUNK, RET_DV)
    qb = q.astype(BF16)
    sc = _bdot_nt(qb, kk.astype(BF16)) * jnp.exp(lg * dist)
    out = _bdot_nn(sc.astype(BF16), v)
    out = out + _bdot_nn(qb, state.astype(BF16)) * jnp.exp(lg * (rowf + 1.0))
    kd = kk * jnp.exp(lg * ((CHUNK - 1.0) - rowf))
    upd = _bdot_nn(jnp.swapaxes(kd, 1, 2).astype(BF16), v)
    new_state = jnp.exp(lg * float(CHUNK)) * state + upd
    mu = jnp.mean(out, axis=-1, keepdims=True)
    cen = out - mu
    var = jnp.mean(cen * cen, axis=-1, keepdims=True)
    normed = cen * lax.rsqrt(var + EPS) * gret_ref[...]
    for st in range(n_s):
        gate = u_ref[st, :, GR_OFF:GR_OFF + RET_WIDTH]
        flat = jnp.concatenate([normed[st * RET_HEADS + h] for h in range(RET_HEADS)], axis=-1)
        ret = flat * (gate * jax.nn.sigmoid(gate))
        mix_ref[st, :, ATTN_WIDTH:ATTN_WIDTH + RET_WIDTH] = ret.astype(mix_ref.dtype)
    return new_state


def _window_to_rows(win_ref, st):
    return jnp.concatenate([win_ref[st * N_KV_HEADS + k, CHUNK:WINDOW + CHUNK, :]
                            for k in range(N_KV_HEADS)], axis=-1)


def _mixer_prompt_body(u_ref, sink_ref, gq_ref, gk_ref, gret_ref, lg_ref, rope_ref,
                       mix_ref, ko_ref, vo_ref, st_ref, kwin_ref, vwin_ref, *, n_chunks, n_s):
    c = pl.program_id(0)
    nbk = n_s * N_KV_HEADS

    @pl.when(c == 0)
    def _():
        kwin_ref[:, 0:WINDOW, :] = jnp.zeros((nbk, WINDOW, HEAD_DIM), F32)
        vwin_ref[:, 0:WINDOW, :] = jnp.zeros((nbk, WINDOW, HEAD_DIM), F32)
        st_ref[...] = jnp.zeros(st_ref.shape, F32)

    _new_kv(u_ref, gk_ref[...], n_s, kwin_ref, vwin_ref)
    key_chunk = lax.broadcasted_iota(jnp.int32, (1, 1, WINDOW + CHUNK), 2) // CHUNK
    valid = (c + key_chunk) >= (WINDOW // CHUNK)
    _attention_chunk(u_ref, kwin_ref, vwin_ref, sink_ref, gq_ref[...], valid, mix_ref, n_s)

    state = st_ref[...].reshape(n_s * RET_HEADS, RET_DK, RET_DV)
    new_state = _retention_chunk(u_ref, rope_ref, lg_ref, gret_ref, c * CHUNK, state, mix_ref, n_s)
    st_ref[...] = new_state.reshape(st_ref.shape)

    @pl.when(c == n_chunks - 1)
    def _():
        for st in range(n_s):
            ko_ref[st] = _window_to_rows(kwin_ref, st)
            vo_ref[st] = _window_to_rows(vwin_ref, st)

    kshift = kwin_ref[:, CHUNK:WINDOW + CHUNK, :]
    vshift = vwin_ref[:, CHUNK:WINDOW + CHUNK, :]
    kwin_ref[:, 0:WINDOW, :] = kshift
    vwin_ref[:, 0:WINDOW, :] = vshift


def _const_spec(shape):
    return pl.BlockSpec(shape, lambda *_: (0,) * len(shape))


def _mixer_prompt(u, sink_col, gq, gk, gret, lg, rope, batch, seq):
    n_chunks = seq // CHUNK
    n_s = batch
    est = (2 * n_s * CHUNK * IN_WIDTH * 4 + 2 * n_s * RET_HEADS * RET_DK * RET_DV * 4
           + 16 * 1024 * 1024)
    return pl.pallas_call(
        functools.partial(_mixer_prompt_body, n_chunks=n_chunks, n_s=n_s),
        out_shape=(
            jax.ShapeDtypeStruct((batch, seq, D_MODEL), BF16),
            jax.ShapeDtypeStruct((batch, WINDOW, KV_COLS), F32),
            jax.ShapeDtypeStruct((batch, WINDOW, KV_COLS), F32),
            jax.ShapeDtypeStruct((batch, RET_HEADS, RET_DK, RET_DV), F32),
        ),
        grid=(n_chunks,),
        in_specs=[
            pl.BlockSpec((n_s, CHUNK, IN_WIDTH), lambda c: (0, c, 0)),
            _const_spec(sink_col.shape),
            _const_spec(gq.shape),
            _const_spec(gk.shape),
            _const_spec(gret.shape),
            _const_spec(lg.shape),
            _const_spec(rope.shape),
        ],
        out_specs=(
            pl.BlockSpec((n_s, CHUNK, D_MODEL), lambda c: (0, c, 0)),
            _const_spec((batch, WINDOW, KV_COLS)),
            _const_spec((batch, WINDOW, KV_COLS)),
            _const_spec((batch, RET_HEADS, RET_DK, RET_DV)),
        ),
        scratch_shapes=[pltpu.VMEM((n_s * N_KV_HEADS, WINDOW + CHUNK, HEAD_DIM), F32),
                        pltpu.VMEM((n_s * N_KV_HEADS, WINDOW + CHUNK, HEAD_DIM), F32)],
        compiler_params=_params(("arbitrary",), est),
        name="mixer_prompt",
    )(u.reshape(batch, seq, IN_WIDTH), sink_col, gq, gk, gret, lg, rope)


def _mixer_sample_body(u_ref, ck_ref, cv_ref, sin_ref, sink_ref, gq_ref, gk_ref, gret_ref,
                       lg_ref, rope_ref, mix_ref, ko_ref, vo_ref, st_ref, kwin_ref, vwin_ref,
                       *, n_s):
    for st in range(n_s):
        for k in range(N_KV_HEADS):
            cols = slice(k * HEAD_DIM, (k + 1) * HEAD_DIM)
            kwin_ref[st * N_KV_HEADS + k, 0:WINDOW, :] = ck_ref[st, :, cols]
            vwin_ref[st * N_KV_HEADS + k, 0:WINDOW, :] = cv_ref[st, :, cols]
    _new_kv(u_ref, gk_ref[...], n_s, kwin_ref, vwin_ref)
    _attention_chunk(u_ref, kwin_ref, vwin_ref, sink_ref, gq_ref[...], None, mix_ref, n_s)
    state = sin_ref[...].reshape(n_s * RET_HEADS, RET_DK, RET_DV)
    new_state = _retention_chunk(u_ref, rope_ref, lg_ref, gret_ref, PAST_LEN, state, mix_ref, n_s)
    st_ref[...] = new_state.reshape(st_ref.shape)
    for st in range(n_s):
        ko_ref[st] = _window_to_rows(kwin_ref, st)
        vo_ref[st] = _window_to_rows(vwin_ref, st)


def _mixer_sample(u, cache_k, cache_v, state, sink_col, gq, gk, gret, lg, rope, n_s):
    nb = cache_k.shape[0]
    est = (2 * n_s * CHUNK * IN_WIDTH * 4 + 4 * n_s * RET_HEADS * RET_DK * RET_DV * 4
           + 16 * 1024 * 1024)
    stream3 = lambda i: (i, 0, 0)
    return pl.pallas_call(
        functools.partial(_mixer_sample_body, n_s=n_s),
        out_shape=(
            jax.ShapeDtypeStruct((nb, CHUNK, D_MODEL), BF16),
            jax.ShapeDtypeStruct((nb, WINDOW, KV_COLS), F32),
            jax.ShapeDtypeStruct((nb, WINDOW, KV_COLS), F32),
            jax.ShapeDtypeStruct((nb, RET_HEADS, RET_DK, RET_DV), F32),
        ),
        grid=(nb // n_s,),
        in_specs=[
            pl.BlockSpec((n_s, CHUNK, IN_WIDTH), stream3),
            pl.BlockSpec((n_s, WINDOW, KV_COLS), stream3),
            pl.BlockSpec((n_s, WINDOW, KV_COLS), stream3),
            pl.BlockSpec((n_s, RET_HEADS, RET_DK, RET_DV), lambda i: (i, 0, 0, 0)),
            _const_spec(sink_col.shape),
            _const_spec(gq.shape),
            _const_spec(gk.shape),
            _const_spec(gret.shape),
            _const_spec(lg.shape),
            _const_spec(rope.shape),
        ],
        out_specs=(
            pl.BlockSpec((n_s, CHUNK, D_MODEL), stream3),
            pl.BlockSpec((n_s, WINDOW, KV_COLS), stream3),
            pl.BlockSpec((n_s, WINDOW, KV_COLS), stream3),
            pl.BlockSpec((n_s, RET_HEADS, RET_DK, RET_DV), lambda i: (i, 0, 0, 0)),
        ),
        scratch_shapes=[pltpu.VMEM((n_s * N_KV_HEADS, WINDOW + CHUNK, HEAD_DIM), F32),
                        pltpu.VMEM((n_s * N_KV_HEADS, WINDOW + CHUNK, HEAD_DIM), F32)],
        compiler_params=_params(("arbitrary",), est),
        name="mixer_sample",
    )(u.reshape(nb, CHUNK, IN_WIDTH), cache_k, cache_v, state, sink_col, gq, gk, gret, lg, rope)


def _mixer_tables(sinks, gret, n_s):
    sink_col = jnp.repeat(sinks.astype(F32), CHUNK).reshape(N_KV_HEADS, GROUP * CHUNK, 1)
    sink_col = jnp.tile(sink_col, (n_s, 1, 1))
    gret3 = jnp.tile(gret.astype(F32).reshape(RET_HEADS, 1, RET_DV), (n_s, 1, 1))
    lg = jnp.tile(jnp.asarray(np.array(LOG_GAMMA, np.float32).reshape(RET_HEADS, 1, 1)),
                  (n_s, 1, 1))
    return sink_col, gret3, lg


def _rope_table():
    half = RET_DK // 2
    inv_freq = (np.float32(ROPE_BASE) ** (-np.arange(half, dtype=np.float32) / np.float32(half)))
    inv_freq = inv_freq.astype(np.float32)
    sign = np.concatenate([-np.ones(half, np.float32), np.ones(half, np.float32)])
    return jnp.asarray(np.stack([np.concatenate([inv_freq, inv_freq]), sign]))


def kernel(x_prompt, x_sample, cache_attn_k, cache_attn_v, state_ret, p_prompt, p_sample, g_ffn1, w_ffn1_gate, w_ffn1_up, w_ffn1_down, g_mix, w_in, g_q, g_k, attn_sinks, g_ret, w_out, g_ffn2, w_ffn2_gate, w_ffn2_up, w_ffn2_down, g_ple, w_ple_gate, w_ple_proj):
    depth = g_ffn1.shape[0]
    batch, seq, d = x_prompt.shape
    nb, dec_len, _ = x_sample.shape
    assert d == D_MODEL and dec_len == CHUNK and seq % CHUNK == 0
    assert w_in.shape[2] == IN_WIDTH and nb % SAMPLE_STREAMS == 0
    rope = _rope_table()

    hp = x_prompt.reshape(batch * seq, d)
    hs = x_sample.reshape(nb * dec_len, d)
    kp_l, vp_l, sp_l, ks_l, vs_l, ss_l = [], [], [], [], [], []
    for i in range(depth):
        wg1 = w_ffn1_gate[i].astype(BF16)
        wu1 = w_ffn1_up[i].astype(BF16)
        wd1 = w_ffn1_down[i].astype(BF16)
        gf1 = g_ffn1[i].reshape(1, d)
        gf2 = g_ffn2[i].reshape(1, d)
        gmx = g_mix[i].reshape(1, d)
        gpl = g_ple[i].reshape(1, d)
        gq = g_q[i].reshape(1, HEAD_DIM)
        gk = g_k[i].reshape(1, HEAD_DIM)
        sinks = attn_sinks[i]
        gret = g_ret[i]

        later = ((w_ffn2_gate[i], FFN_TF), (w_ffn2_up[i], FFN_TF), (w_ffn2_down[i], d),
                 (w_in[i], PROJ_TN), (w_out[i], MERGE_TN), (w_ple_gate[i], PLE_TN),
                 (w_ple_proj[i], PLE_TN))
        hp, zp, (wg2, wu2, wd2, win, wout, wpg, wpp) = _ffn(hp, gf1, wg1, wu1, wd1, gmx, later)
        wd2 = wd2.reshape(wd2.shape[1], d)
        hs, zs, _ = _ffn(hs, gf1, wg1, wu1, wd1, gmx)

        up = _matmul(zp, win, PROJ_TM, name="proj")
        us = _matmul(zs, win, PROJ_TM, name="proj")
        sink_p, gret_p, lg_p = _mixer_tables(sinks, gret, batch)
        sink_s, gret_s, lg_s = _mixer_tables(sinks, gret, SAMPLE_STREAMS)
        mixp, kp, vp, sp = _mixer_prompt(up, sink_p, gq, gk, gret_p, lg_p, rope, batch, seq)
        mixs, ks, vs, ss = _mixer_sample(
            us, cache_attn_k[i].reshape(nb, WINDOW, KV_COLS),
            cache_attn_v[i].reshape(nb, WINDOW, KV_COLS), state_ret[i],
            sink_s, gq, gk, gret_s, lg_s, rope, SAMPLE_STREAMS)
        mixp = mixp.reshape(batch * seq, d)
        mixs = mixs.reshape(nb * dec_len, d)
        hp = _matmul(mixp, wout, MERGE_TM, h=hp, name="merge")
        hs = _matmul(mixs, wout, MERGE_TM, h=hs, name="merge")
        kp_l.append(kp.reshape(batch, WINDOW, N_KV_HEADS, HEAD_DIM))
        vp_l.append(vp.reshape(batch, WINDOW, N_KV_HEADS, HEAD_DIM))
        sp_l.append(sp)
        ks_l.append(ks.reshape(nb, WINDOW, N_KV_HEADS, HEAD_DIM))
        vs_l.append(vs.reshape(nb, WINDOW, N_KV_HEADS, HEAD_DIM))
        ss_l.append(ss)

        hp, zp, _ = _ffn(hp, gf2, wg2, wu2, wd2, gpl)
        hs, zs, _ = _ffn(hs, gf2, wg2, wu2, wd2, gpl)

        hp = _matmul(zp, wpg, PLE_TM, h=hp, p=p_prompt[i].reshape(batch * seq, -1),
                     wp=wpp, name="ple")
        hs = _matmul(zs, wpg, PLE_TM, h=hs, p=p_sample[i].reshape(nb * dec_len, -1),
                     wp=wpp, name="ple")

    return (hp.reshape(batch, seq, d), hs.reshape(nb, dec_len, d),
            jnp.stack(kp_l), jnp.stack(vp_l), jnp.stack(sp_l),
            jnp.stack(ks_l), jnp.stack(vs_l), jnp.stack(ss_l))
```

```python
import functools
import math

import numpy as np
import jax
import jax.numpy as jnp
from jax import lax
from jax.experimental import pallas as pl
from jax.experimental.pallas import tpu as pltpu

F32 = jnp.float32
BF16 = jnp.bfloat16

D_MODEL = 4096
CHUNK = 64
HEAD_DIM = 64
N_HEADS = 32
N_KV_HEADS = 4
GROUP = N_HEADS // N_KV_HEADS
WINDOW = 128
RET_HEADS = 8
RET_DK = 128
RET_DV = 256
ATTN_WIDTH = N_HEADS * HEAD_DIM
RET_WIDTH = RET_HEADS * RET_DV
PAST_LEN = 4096
ROPE_BASE = 10000.0
EPS = 1e-6

Q_OFF = 0
K_OFF = Q_OFF + N_HEADS * HEAD_DIM
V_OFF = K_OFF + N_KV_HEADS * HEAD_DIM
QR_OFF = V_OFF + N_KV_HEADS * HEAD_DIM
KR_OFF = QR_OFF + RET_HEADS * RET_DK
VR_OFF = KR_OFF + RET_HEADS * RET_DK
GR_OFF = VR_OFF + RET_WIDTH
IN_WIDTH = GR_OFF + RET_WIDTH
KV_COLS = N_KV_HEADS * HEAD_DIM

LOG2_E = math.log2(math.e)
LOG_GAMMA = tuple(math.log(1.0 - 2.0 ** (-5.0 - h)) for h in range(RET_HEADS))

VMEM_CAP_BYTES = 60000 * 1024

FFN_TM = 1024
FFN_TF = 256
PROJ_TM, PROJ_TN = 2048, 512
MERGE_TM, MERGE_TN = 1024, 1024
PLE_TM, PLE_TN = 1024, 512
NORM_ROWS = 16
SAMPLE_STREAMS = 2
ROW_LOOP_UNROLL = 8
X_COPY_CHUNKS = 4


def _vmem_limit(estimate_bytes):
    return int(min(VMEM_CAP_BYTES, max(32 * 1024 * 1024, estimate_bytes * 5 // 4)))


def _params(semantics, vmem_estimate):
    return pltpu.CompilerParams(dimension_semantics=semantics,
                                vmem_limit_bytes=_vmem_limit(vmem_estimate))


def _row_loop(n_rows, body):
    def step(r, carry):
        body(pl.ds(pl.multiple_of(r * NORM_ROWS, NORM_ROWS), NORM_ROWS))
        return carry

    lax.fori_loop(0, n_rows // NORM_ROWS, step, 0, unroll=ROW_LOOP_UNROLL)


def _rms_rows(x, g):
    ms = jnp.mean(x * x, axis=-1, keepdims=True)
    return x * lax.rsqrt(ms + EPS) * g


def _cast_plan(shape, bc, n_steps):
    r, c = shape
    nc = c // bc
    fits = [nr for nr in range(1, r + 1)
            if r % nr == 0 and (r // nr) % 16 == 0 and nr * nc <= n_steps]
    if not fits:
        return None
    nr = max(fits)
    return r // nr, bc, nr, nc


def _ffn_body(*refs, n_i, n_f, tm, n_side):
    x_hbm, g_ref, gn_ref, wg_ref, wu_ref, wd_ref = refs[:6]
    side_in = refs[6:6 + n_side]
    o_hbm, zn_ref = refs[6 + n_side:8 + n_side]
    side_out = refs[8 + n_side:8 + 2 * n_side]
    o_ref, x_sems, o_sems = refs[8 + 2 * n_side:]
    i = pl.program_id(0)
    f = pl.program_id(1)
    rc = tm // X_COPY_CHUNKS

    def x_copy(c):
        return pltpu.make_async_copy(
            x_hbm.at[pl.ds(pl.multiple_of(i * tm, tm) + c * rc, rc), :],
            o_ref.at[pl.ds(c * rc, rc), :], x_sems.at[c])

    def o_copy(c, tile):
        return pltpu.make_async_copy(
            o_ref.at[pl.ds(c * rc, rc), :],
            o_hbm.at[pl.ds(pl.multiple_of(tile * tm, tm) + c * rc, rc), :], o_sems.at[c])

    @pl.when(f == 0)
    def _():
        for c in range(X_COPY_CHUNKS):
            @pl.when(i > 0)
            def _(c=c):
                o_copy(c, i - 1).wait()

            x_copy(c).start()
        g = g_ref[...]
        for c in range(X_COPY_CHUNKS):
            x_copy(c).wait()

            def init(rows, base=c * rc):
                rows = pl.ds(rows.start + base, NORM_ROWS)
                zn_ref[rows, :] = _rms_rows(o_ref[rows, :], g).astype(zn_ref.dtype)

            _row_loop(rc, init)

    for src, dst in zip(side_in, side_out):
        dst[...] = src[...].astype(dst.dtype)

    z = zn_ref[...]
    gate = jnp.dot(z, wg_ref[...], preferred_element_type=F32)
    up = jnp.dot(z, wu_ref[...], preferred_element_type=F32)
    a = (0.5 * gate * jax.nn.sigmoid(gate) * up).astype(BF16)
    o_ref[...] += jnp.dot(a, wd_ref[...], preferred_element_type=F32)

    @pl.when(f == n_f - 1)
    def _():
        for c in range(X_COPY_CHUNKS):
            o_copy(c, i).start()
        gn = gn_ref[...]

        def fin(rows):
            zn_ref[rows, :] = _rms_rows(o_ref[rows, :], gn).astype(zn_ref.dtype)

        _row_loop(tm, fin)

        @pl.when(i == n_i - 1)
        def _():
            for c in range(X_COPY_CHUNKS):
                o_copy(c, i).wait()


def _ffn(x, g, wg, wu, wd, g_next, side=()):
    m, d = x.shape
    tm = min(FFN_TM, m)
    tile_major = wg.ndim == 3
    tf = wg.shape[2] if tile_major else FFN_TF
    n_f = wg.shape[0] if tile_major else wg.shape[1] // tf
    n_i = m // tm
    if tile_major:
        w_spec = pl.BlockSpec((None, d, tf), lambda i, f: (f, 0, 0))
    else:
        w_spec = pl.BlockSpec((d, tf), lambda i, f: (0, f))
    assert tm % (X_COPY_CHUNKS * NORM_ROWS) == 0
    plans = [_cast_plan(w.shape, bc, n_i * n_f) for w, bc in side]
    assert all(p is not None for p in plans)

    def side_specs(plan):
        br, bc, nr, nc = plan

        def block(i, f):
            b = jnp.minimum(i * n_f + f, nr * nc - 1)
            return b // nc, b % nc

        def src_index(i, f):
            return block(i, f)

        def dst_index(i, f):
            rb, cb = block(i, f)
            return cb, rb, 0

        return pl.BlockSpec((br, bc), src_index), pl.BlockSpec((None, br, bc), dst_index)

    specs = [side_specs(p) for p in plans]
    est = (tm * d * 4 + 2 * tm * d * 2 + 2 * 3 * d * tf * 2
           + sum(2 * p[0] * p[1] * 6 for p in plans))
    outs = pl.pallas_call(
        functools.partial(_ffn_body, n_i=n_i, n_f=n_f, tm=tm, n_side=len(side)),
        out_shape=(jax.ShapeDtypeStruct((m, d), F32), jax.ShapeDtypeStruct((m, d), BF16))
        + tuple(jax.ShapeDtypeStruct((w.shape[1] // bc, w.shape[0], bc), BF16)
                for w, bc in side),
        grid=(n_i, n_f),
        in_specs=[
            pl.BlockSpec(memory_space=pl.ANY),
            pl.BlockSpec((1, d), lambda i, f: (0, 0)),
            pl.BlockSpec((1, d), lambda i, f: (0, 0)),
            w_spec,
            w_spec,
            pl.BlockSpec((tf, d), lambda i, f: (f, 0)),
        ] + [src for src, _ in specs],
        out_specs=(
            pl.BlockSpec(memory_space=pl.ANY),
            pl.BlockSpec((tm, d), lambda i, f: (i, 0)),
        ) + tuple(dst for _, dst in specs),
        scratch_shapes=[pltpu.VMEM((tm, d), F32),
                        pltpu.SemaphoreType.DMA((X_COPY_CHUNKS,)),
                        pltpu.SemaphoreType.DMA((X_COPY_CHUNKS,))],
        compiler_params=_params(("arbitrary", "arbitrary"), est),
        name="ffn",
    )(x, g, g_next, wg, wu, wd, *[w for w, _ in side])
    return outs[0], outs[1], outs[2:]


def _matmul_body(*refs, has_h, has_p):
    z_ref, w_ref = refs[:2]
    o_ref = refs[-1]
    acc = jnp.dot(z_ref[...], w_ref[...], preferred_element_type=F32)
    if has_p:
        h_ref, p_ref, wp_ref = refs[2:5]
        proj = jnp.dot(p_ref[...].astype(BF16), wp_ref[...], preferred_element_type=F32)
        o_ref[...] = h_ref[...] + jax.nn.sigmoid(acc) * proj
    elif has_h:
        o_ref[...] = refs[2][...] + acc
    else:
        o_ref[...] = acc


def _matmul(z, w, tm, h=None, p=None, wp=None, name="matmul"):
    m, k = z.shape
    n_j, _, tn = w.shape
    n = n_j * tn
    tm = min(tm, m)
    operands = [z, w]
    in_specs = [pl.BlockSpec((tm, k), lambda i, j: (i, 0)),
                pl.BlockSpec((None, k, tn), lambda i, j: (j, 0, 0))]
    est = 2 * tm * k * 2 + 2 * k * tn * 2 + 2 * tm * tn * 4
    if h is not None:
        operands.append(h)
        in_specs.append(pl.BlockSpec((tm, tn), lambda i, j: (i, j)))
        est += 2 * tm * tn * 4
    if p is not None:
        pd = p.shape[1]
        operands += [p, wp]
        in_specs += [pl.BlockSpec((tm, pd), lambda i, j: (i, 0)),
                     pl.BlockSpec((None, pd, tn), lambda i, j: (j, 0, 0))]
        est += 2 * tm * pd * 4 + 2 * pd * tn * 2
    return pl.pallas_call(
        functools.partial(_matmul_body, has_h=h is not None, has_p=p is not None),
        out_shape=jax.ShapeDtypeStruct((m, n), F32),
        grid=(m // tm, n_j),
        in_specs=in_specs,
        out_specs=pl.BlockSpec((tm, tn), lambda i, j: (i, j)),
        compiler_params=_params(("arbitrary", "arbitrary"), est),
        name=name,
    )(*operands)


def _stack_cols(u_ref, n_s, off, width, count):
    return jnp.concatenate(
        [u_ref[s, :, off + j * width:off + (j + 1) * width]
         for s in range(n_s) for j in range(count)], axis=0)


def _bdot_nt(a, b):
    return lax.dot_general(a, b, (((2,), (2,)), ((0,), (0,))), preferred_element_type=F32)


def _bdot_nn(a, b):
    return lax.dot_general(a, b, (((2,), (1,)), ((0,), (0,))), preferred_element_type=F32)


def _new_kv(u_ref, gk, n_s, kwin_ref, vwin_ref):
    nb = n_s * N_KV_HEADS
    knew = _rms_rows(_stack_cols(u_ref, n_s, K_OFF, HEAD_DIM, N_KV_HEADS), gk)
    vnew = _stack_cols(u_ref, n_s, V_OFF, HEAD_DIM, N_KV_HEADS)
    kwin_ref[:, WINDOW:WINDOW + CHUNK, :] = knew.reshape(nb, CHUNK, HEAD_DIM)
    vwin_ref[:, WINDOW:WINDOW + CHUNK, :] = vnew.reshape(nb, CHUNK, HEAD_DIM)


def _attention_chunk(u_ref, kwin_ref, vwin_ref, sink_ref, gq, c_first, mix_ref, n_s):
    nb = n_s * N_KV_HEADS
    nq = GROUP * CHUNK
    q = _rms_rows(_stack_cols(u_ref, n_s, Q_OFF, HEAD_DIM, N_HEADS), gq)
    q = q.astype(BF16).reshape(nb, nq, HEAD_DIM)
    s = _bdot_nt(kwin_ref[...].astype(BF16), q) * (HEAD_DIM ** -0.5 * LOG2_E)
    if c_first is not None:
        key_chunk = lax.broadcasted_iota(jnp.int32, (1, WINDOW + CHUNK, nq), 1) // CHUNK
        s = jnp.where((c_first + key_chunk) >= (WINDOW // CHUNK), s, -jnp.inf)
    sink = sink_ref[...] * LOG2_E
    m = jnp.maximum(jnp.max(s, axis=1, keepdims=True), sink)
    e = jnp.exp2(s - m)
    denom = jnp.sum(e, axis=1, keepdims=True) + jnp.exp2(sink - m)
    v_t = jnp.swapaxes(vwin_ref[...], 1, 2).astype(BF16)
    o_t = _bdot_nn(v_t, e.astype(BF16)) / denom
    o = jnp.swapaxes(o_t, 1, 2)
    for st in range(n_s):
        pieces = [o[st * N_KV_HEADS + k, g * CHUNK:(g + 1) * CHUNK, :]
                  for k in range(N_KV_HEADS) for g in range(GROUP)]
        mix_ref[st, :, 0:ATTN_WIDTH] = jnp.concatenate(pieces, axis=-1).astype(mix_ref.dtype)


def _retention_chunk(u_ref, rope_ref, lg_ref, gret_ref, pos0, state, mix_ref, n_s):
    nb = n_s * RET_HEADS
    row = lax.broadcasted_iota(jnp.int32, (CHUNK, 1), 0)
    pos = (pos0 + row).astype(F32)
    ang = pos * rope_ref[0:1, :]
    cosv = jnp.cos(ang)[None]
    sinv = (jnp.sin(ang) * rope_ref[1:2, :])[None]
    lg = lg_ref[...]
    ri = lax.broadcasted_iota(jnp.int32, (CHUNK, CHUNK), 0)
    ci = lax.broadcasted_iota(jnp.int32, (CHUNK, CHUNK), 1)
    dist = jnp.abs(ri - ci).astype(F32)[None]
    rowf = row.astype(F32)[None]

    def rope(x):
        rolled = pltpu.roll(x, RET_DK // 2, 1)
        return (x.reshape(nb, CHUNK, RET_DK) * cosv
                + rolled.reshape(nb, CHUNK, RET_DK) * sinv)

    q = rope(_stack_cols(u_ref, n_s, QR_OFF, RET_DK, RET_HEADS))
    kk = rope(_stack_cols(u_ref, n_s, KR_OFF, RET_DK, RET_HEADS)) * (RET_DK ** -0.5)
    v = _stack_cols(u_ref, n_s, VR_OFF, RET_DV, RET_HEADS).astype(BF16)
    v = v.reshape(nb, CHUNK, RET_DV)
    qb = q.astype(BF16)
    sc = _bdot_nt(qb, kk.astype(BF16)) * jnp.exp(lg * dist)
    out = _bdot_nn(sc.astype(BF16), v)
    out = out + _bdot_nn(qb, state.astype(BF16)) * jnp.exp(lg * (rowf + 1.0))
    kd = kk * jnp.exp(lg * ((CHUNK - 1.0) - rowf))
    upd = _bdot_nn(jnp.swapaxes(kd, 1, 2).astype(BF16), v)
    new_state = jnp.exp(lg * float(CHUNK)) * state + upd
    mu = jnp.mean(out, axis=-1, keepdims=True)
    cen = out - mu
    var = jnp.mean(cen * cen, axis=-1, keepdims=True)
    normed = cen * lax.rsqrt(var + EPS) * gret_ref[...]
    for st in range(n_s):
        gate = u_ref[st, :, GR_OFF:GR_OFF + RET_WIDTH]
        flat = jnp.concatenate([normed[st * RET_HEADS + h] for h in range(RET_HEADS)], axis=-1)
        ret = flat * (gate * jax.nn.sigmoid(gate))
        mix_ref[st, :, ATTN_WIDTH:ATTN_WIDTH + RET_WIDTH] = ret.astype(mix_ref.dtype)
    return new_state


def _window_to_rows(win_ref, st):
    return jnp.concatenate([win_ref[st * N_KV_HEADS + k, CHUNK:WINDOW + CHUNK, :]
                            for k in range(N_KV_HEADS)], axis=-1)


def _mixer_prompt_body(u_ref, sink_ref, gq_ref, gk_ref, gret_ref, lg_ref, rope_ref,
                       mix_ref, ko_ref, vo_ref, st_ref, kwin_ref, vwin_ref, *, n_chunks, n_s):
    c = pl.program_id(0)
    nbk = n_s * N_KV_HEADS

    @pl.when(c == 0)
    def _():
        kwin_ref[:, 0:WINDOW, :] = jnp.zeros((nbk, WINDOW, HEAD_DIM), F32)
        vwin_ref[:, 0:WINDOW, :] = jnp.zeros((nbk, WINDOW, HEAD_DIM), F32)
        st_ref[...] = jnp.zeros(st_ref.shape, F32)

    _new_kv(u_ref, gk_ref[...], n_s, kwin_ref, vwin_ref)
    _attention_chunk(u_ref, kwin_ref, vwin_ref, sink_ref, gq_ref[...], c, mix_ref, n_s)

    state = st_ref[...].reshape(n_s * RET_HEADS, RET_DK, RET_DV)
    new_state = _retention_chunk(u_ref, rope_ref, lg_ref, gret_ref, c * CHUNK, state, mix_ref, n_s)
    st_ref[...] = new_state.reshape(st_ref.shape)

    @pl.when(c == n_chunks - 1)
    def _():
        for st in range(n_s):
            ko_ref[st] = _window_to_rows(kwin_ref, st)
            vo_ref[st] = _window_to_rows(vwin_ref, st)

    kshift = kwin_ref[:, CHUNK:WINDOW + CHUNK, :]
    vshift = vwin_ref[:, CHUNK:WINDOW + CHUNK, :]
    kwin_ref[:, 0:WINDOW, :] = kshift
    vwin_ref[:, 0:WINDOW, :] = vshift


def _const_spec(shape):
    return pl.BlockSpec(shape, lambda *_: (0,) * len(shape))


def _mixer_prompt(u, sink_col, gq, gk, gret, lg, rope, batch, seq):
    n_chunks = seq // CHUNK
    n_s = batch
    est = (2 * n_s * CHUNK * IN_WIDTH * 4 + 2 * n_s * RET_HEADS * RET_DK * RET_DV * 4
           + 16 * 1024 * 1024)
    return pl.pallas_call(
        functools.partial(_mixer_prompt_body, n_chunks=n_chunks, n_s=n_s),
        out_shape=(
            jax.ShapeDtypeStruct((batch, seq, D_MODEL), BF16),
            jax.ShapeDtypeStruct((batch, WINDOW, KV_COLS), F32),
            jax.ShapeDtypeStruct((batch, WINDOW, KV_COLS), F32),
            jax.ShapeDtypeStruct((batch, RET_HEADS, RET_DK, RET_DV), F32),
        ),
        grid=(n_chunks,),
        in_specs=[
            pl.BlockSpec((n_s, CHUNK, IN_WIDTH), lambda c: (0, c, 0)),
            _const_spec(sink_col.shape),
            _const_spec(gq.shape),
            _const_spec(gk.shape),
            _const_spec(gret.shape),
            _const_spec(lg.shape),
            _const_spec(rope.shape),
        ],
        out_specs=(
            pl.BlockSpec((n_s, CHUNK, D_MODEL), lambda c: (0, c, 0)),
            _const_spec((batch, WINDOW, KV_COLS)),
            _const_spec((batch, WINDOW, KV_COLS)),
            _const_spec((batch, RET_HEADS, RET_DK, RET_DV)),
        ),
        scratch_shapes=[pltpu.VMEM((n_s * N_KV_HEADS, WINDOW + CHUNK, HEAD_DIM), F32),
                        pltpu.VMEM((n_s * N_KV_HEADS, WINDOW + CHUNK, HEAD_DIM), F32)],
        compiler_params=_params(("arbitrary",), est),
        name="mixer_prompt",
    )(u.reshape(batch, seq, IN_WIDTH), sink_col, gq, gk, gret, lg, rope)


def _mixer_sample_body(u_ref, ck_ref, cv_ref, sin_ref, sink_ref, gq_ref, gk_ref, gret_ref,
                       lg_ref, rope_ref, mix_ref, ko_ref, vo_ref, st_ref, kwin_ref, vwin_ref,
                       *, n_s):
    for st in range(n_s):
        for k in range(N_KV_HEADS):
            cols = slice(k * HEAD_DIM, (k + 1) * HEAD_DIM)
            kwin_ref[st * N_KV_HEADS + k, 0:WINDOW, :] = ck_ref[st, :, cols]
            vwin_ref[st * N_KV_HEADS + k, 0:WINDOW, :] = cv_ref[st, :, cols]
    _new_kv(u_ref, gk_ref[...], n_s, kwin_ref, vwin_ref)
    _attention_chunk(u_ref, kwin_ref, vwin_ref, sink_ref, gq_ref[...], None, mix_ref, n_s)
    state = sin_ref[...].reshape(n_s * RET_HEADS, RET_DK, RET_DV)
    new_state = _retention_chunk(u_ref, rope_ref, lg_ref, gret_ref, PAST_LEN, state, mix_ref, n_s)
    st_ref[...] = new_state.reshape(st_ref.shape)
    for st in range(n_s):
        ko_ref[st] = _window_to_rows(kwin_ref, st)
        vo_ref[st] = _window_to_rows(vwin_ref, st)


def _mixer_sample(u, cache_k, cache_v, state, sink_col, gq, gk, gret, lg, rope, n_s):
    nb = cache_k.shape[0]
    est = (2 * n_s * CHUNK * IN_WIDTH * 4 + 4 * n_s * RET_HEADS * RET_DK * RET_DV * 4
           + 16 * 1024 * 1024)
    stream3 = lambda i: (i, 0, 0)
    return pl.pallas_call(
        functools.partial(_mixer_sample_body, n_s=n_s),
        out_shape=(
            jax.ShapeDtypeStruct((nb, CHUNK, D_MODEL), BF16),
            jax.ShapeDtypeStruct((nb, WINDOW, KV_COLS), F32),
            jax.ShapeDtypeStruct((nb, WINDOW, KV_COLS), F32),
            jax.ShapeDtypeStruct((nb, RET_HEADS, RET_DK, RET_DV), F32),
        ),
        grid=(nb // n_s,),
        in_specs=[
            pl.BlockSpec((n_s, CHUNK, IN_WIDTH), stream3),
            pl.BlockSpec((n_s, WINDOW, KV_COLS), stream3),
            pl.BlockSpec((n_s, WINDOW, KV_COLS), stream3),
            pl.BlockSpec((n_s, RET_HEADS, RET_DK, RET_DV), lambda i: (i, 0, 0, 0)),
            _const_spec(sink_col.shape),
            _const_spec(gq.shape),
            _const_spec(gk.shape),
            _const_spec(gret.shape),
            _const_spec(lg.shape),
            _const_spec(rope.shape),
        ],
        out_specs=(
            pl.BlockSpec((n_s, CHUNK, D_MODEL), stream3),
            pl.BlockSpec((n_s, WINDOW, KV_COLS), stream3),
            pl.BlockSpec((n_s, WINDOW, KV_COLS), stream3),
            pl.BlockSpec((n_s, RET_HEADS, RET_DK, RET_DV), lambda i: (i, 0, 0, 0)),
        ),
        scratch_shapes=[pltpu.VMEM((n_s * N_KV_HEADS, WINDOW + CHUNK, HEAD_DIM), F32),
                        pltpu.VMEM((n_s * N_KV_HEADS, WINDOW + CHUNK, HEAD_DIM), F32)],
        compiler_params=_params(("arbitrary",), est),
        name="mixer_sample",
    )(u.reshape(nb, CHUNK, IN_WIDTH), cache_k, cache_v, state, sink_col, gq, gk, gret, lg, rope)


def _mixer_tables(sinks, gret, n_s):
    sink_col = jnp.repeat(sinks.astype(F32), CHUNK).reshape(N_KV_HEADS, 1, GROUP * CHUNK)
    sink_col = jnp.tile(sink_col, (n_s, 1, 1))
    gret3 = jnp.tile(gret.astype(F32).reshape(RET_HEADS, 1, RET_DV), (n_s, 1, 1))
    lg = jnp.tile(jnp.asarray(np.array(LOG_GAMMA, np.float32).reshape(RET_HEADS, 1, 1)),
                  (n_s, 1, 1))
    return sink_col, gret3, lg


def _rope_table():
    half = RET_DK // 2
    inv_freq = (np.float32(ROPE_BASE) ** (-np.arange(half, dtype=np.float32) / np.float32(half)))
    inv_freq = inv_freq.astype(np.float32)
    sign = np.concatenate([-np.ones(half, np.float32), np.ones(half, np.float32)])
    return jnp.asarray(np.stack([np.concatenate([inv_freq, inv_freq]), sign]))


def kernel(x_prompt, x_sample, cache_attn_k, cache_attn_v, state_ret, p_prompt, p_sample, g_ffn1, w_ffn1_gate, w_ffn1_up, w_ffn1_down, g_mix, w_in, g_q, g_k, attn_sinks, g_ret, w_out, g_ffn2, w_ffn2_gate, w_ffn2_up, w_ffn2_down, g_ple, w_ple_gate, w_ple_proj):
    depth = g_ffn1.shape[0]
    batch, seq, d = x_prompt.shape
    nb, dec_len, _ = x_sample.shape
    assert d == D_MODEL and dec_len == CHUNK and seq % CHUNK == 0
    assert w_in.shape[2] == IN_WIDTH and nb % SAMPLE_STREAMS == 0
    rope = _rope_table()

    hp = x_prompt.reshape(batch * seq, d)
    hs = x_sample.reshape(nb * dec_len, d)
    kp_l, vp_l, sp_l, ks_l, vs_l, ss_l = [], [], [], [], [], []
    for i in range(depth):
        wg1 = w_ffn1_gate[i].astype(BF16)
        wu1 = w_ffn1_up[i].astype(BF16)
        wd1 = w_ffn1_down[i].astype(BF16)
        gf1 = g_ffn1[i].reshape(1, d)
        gf2 = g_ffn2[i].reshape(1, d)
        gmx = g_mix[i].reshape(1, d)
        gpl = g_ple[i].reshape(1, d)
        gq = g_q[i].reshape(1, HEAD_DIM)
        gk = g_k[i].reshape(1, HEAD_DIM)
        sinks = attn_sinks[i]
        gret = g_ret[i]

        later = ((w_ffn2_gate[i], FFN_TF), (w_ffn2_up[i], FFN_TF), (w_ffn2_down[i], d),
                 (w_in[i], PROJ_TN), (w_out[i], MERGE_TN), (w_ple_gate[i], PLE_TN),
                 (w_ple_proj[i], PLE_TN))
        hp, zp, (wg2, wu2, wd2, win, wout, wpg, wpp) = _ffn(hp, gf1, wg1, wu1, wd1, gmx, later)
        wd2 = wd2.reshape(wd2.shape[1], d)
        hs, zs, _ = _ffn(hs, gf1, wg1, wu1, wd1, gmx)

        up = _matmul(zp, win, PROJ_TM, name="proj")
        us = _matmul(zs, win, PROJ_TM, name="proj")
        sink_p, gret_p, lg_p = _mixer_tables(sinks, gret, batch)
        sink_s, gret_s, lg_s = _mixer_tables(sinks, gret, SAMPLE_STREAMS)
        mixp, kp, vp, sp = _mixer_prompt(up, sink_p, gq, gk, gret_p, lg_p, rope, batch, seq)
        mixs, ks, vs, ss = _mixer_sample(
            us, cache_attn_k[i].reshape(nb, WINDOW, KV_COLS),
            cache_attn_v[i].reshape(nb, WINDOW, KV_COLS), state_ret[i],
            sink_s, gq, gk, gret_s, lg_s, rope, SAMPLE_STREAMS)
        mixp = mixp.reshape(batch * seq, d)
        mixs = mixs.reshape(nb * dec_len, d)
        hp = _matmul(mixp, wout, MERGE_TM, h=hp, name="merge")
        hs = _matmul(mixs, wout, MERGE_TM, h=hs, name="merge")
        kp_l.append(kp.reshape(batch, WINDOW, N_KV_HEADS, HEAD_DIM))
        vp_l.append(vp.reshape(batch, WINDOW, N_KV_HEADS, HEAD_DIM))
        sp_l.append(sp)
        ks_l.append(ks.reshape(nb, WINDOW, N_KV_HEADS, HEAD_DIM))
        vs_l.append(vs.reshape(nb, WINDOW, N_KV_HEADS, HEAD_DIM))
        ss_l.append(ss)

        hp, zp, _ = _ffn(hp, gf2, wg2, wu2, wd2, gpl)
        hs, zs, _ = _ffn(hs, gf2, wg2, wu2, wd2, gpl)

        hp = _matmul(zp, wpg, PLE_TM, h=hp, p=p_prompt[i].reshape(batch * seq, -1),
                     wp=wpp, name="ple")
        hs = _matmul(zs, wpg, PLE_TM, h=hs, p=p_sample[i].reshape(nb * dec_len, -1),
                     wp=wpp, name="ple")

    return (hp.reshape(batch, seq, d), hs.reshape(nb, dec_len, d),
            jnp.stack(kp_l), jnp.stack(vp_l), jnp.stack(sp_l),
            jnp.stack(ks_l), jnp.stack(vs_l), jnp.stack(ss_l))
```

```python
import functools
import math

import numpy as np
import jax
import jax.numpy as jnp
from jax import lax
from jax.experimental import pallas as pl
from jax.experimental.pallas import tpu as pltpu

F32 = jnp.float32
BF16 = jnp.bfloat16

D_MODEL = 4096
CHUNK = 64
HEAD_DIM = 64
N_HEADS = 32
N_KV_HEADS = 4
GROUP = N_HEADS // N_KV_HEADS
WINDOW = 128
RET_HEADS = 8
RET_DK = 128
RET_DV = 256
ATTN_WIDTH = N_HEADS * HEAD_DIM
RET_WIDTH = RET_HEADS * RET_DV
PAST_LEN = 4096
ROPE_BASE = 10000.0
EPS = 1e-6

Q_OFF = 0
K_OFF = Q_OFF + N_HEADS * HEAD_DIM
V_OFF = K_OFF + N_KV_HEADS * HEAD_DIM
QR_OFF = V_OFF + N_KV_HEADS * HEAD_DIM
KR_OFF = QR_OFF + RET_HEADS * RET_DK
VR_OFF = KR_OFF + RET_HEADS * RET_DK
GR_OFF = VR_OFF + RET_WIDTH
IN_WIDTH = GR_OFF + RET_WIDTH
KV_COLS = N_KV_HEADS * HEAD_DIM

LOG2_E = math.log2(math.e)
LOG_GAMMA = tuple(math.log(1.0 - 2.0 ** (-5.0 - h)) for h in range(RET_HEADS))

VMEM_CAP_BYTES = 60000 * 1024

FFN_TM = 1024
FFN_TF = 256
PROJ_TM, PROJ_TN = 2048, 512
MERGE_TM, MERGE_TN = 1024, 1024
PLE_TM, PLE_TN = 1024, 512
NORM_ROWS = 16
SAMPLE_STREAMS = 2
ROW_LOOP_UNROLL = 8
X_COPY_CHUNKS = 4


def _vmem_limit(estimate_bytes):
    return int(min(VMEM_CAP_BYTES, max(32 * 1024 * 1024, estimate_bytes * 5 // 4)))


def _params(semantics, vmem_estimate):
    return pltpu.CompilerParams(dimension_semantics=semantics,
                                vmem_limit_bytes=_vmem_limit(vmem_estimate))


def _row_loop(n_rows, body):
    def step(r, carry):
        body(pl.ds(pl.multiple_of(r * NORM_ROWS, NORM_ROWS), NORM_ROWS))
        return carry

    lax.fori_loop(0, n_rows // NORM_ROWS, step, 0, unroll=ROW_LOOP_UNROLL)


def _rms_rows(x, g):
    ms = jnp.mean(x * x, axis=-1, keepdims=True)
    return x * lax.rsqrt(ms + EPS) * g


def _cast_plan(shape, bc, n_steps):
    r, c = shape
    nc = c // bc
    fits = [nr for nr in range(1, r + 1)
            if r % nr == 0 and (r // nr) % 16 == 0 and nr * nc <= n_steps]
    if not fits:
        return None
    nr = max(fits)
    return r // nr, bc, nr, nc


def _ffn_body(*refs, n_i, n_f, tm, n_side):
    x_hbm, g_ref, gn_ref, wg_ref, wu_ref, wd_ref = refs[:6]
    side_in = refs[6:6 + n_side]
    o_hbm, zn_ref = refs[6 + n_side:8 + n_side]
    side_out = refs[8 + n_side:8 + 2 * n_side]
    o_ref, x_sems, o_sems = refs[8 + 2 * n_side:]
    i = pl.program_id(0)
    f = pl.program_id(1)
    rc = tm // X_COPY_CHUNKS

    def x_copy(c):
        return pltpu.make_async_copy(
            x_hbm.at[pl.ds(pl.multiple_of(i * tm, tm) + c * rc, rc), :],
            o_ref.at[pl.ds(c * rc, rc), :], x_sems.at[c])

    def o_copy(c, tile):
        return pltpu.make_async_copy(
            o_ref.at[pl.ds(c * rc, rc), :],
            o_hbm.at[pl.ds(pl.multiple_of(tile * tm, tm) + c * rc, rc), :], o_sems.at[c])

    @pl.when(f == 0)
    def _():
        for c in range(X_COPY_CHUNKS):
            @pl.when(i > 0)
            def _(c=c):
                o_copy(c, i - 1).wait()

            x_copy(c).start()
        g = g_ref[...]
        for c in range(X_COPY_CHUNKS):
            x_copy(c).wait()

            def init(rows, base=c * rc):
                rows = pl.ds(rows.start + base, NORM_ROWS)
                zn_ref[rows, :] = _rms_rows(o_ref[rows, :], g).astype(zn_ref.dtype)

            _row_loop(rc, init)

    for src, dst in zip(side_in, side_out):
        dst[...] = src[...].astype(dst.dtype)

    z = zn_ref[...]
    gate = jnp.dot(z, wg_ref[...], preferred_element_type=F32)
    up = jnp.dot(z, wu_ref[...], preferred_element_type=F32)
    a = (0.5 * gate * jax.nn.sigmoid(gate) * up).astype(BF16)
    o_ref[...] += jnp.dot(a, wd_ref[...], preferred_element_type=F32)

    @pl.when(f == n_f - 1)
    def _():
        for c in range(X_COPY_CHUNKS):
            o_copy(c, i).start()
        gn = gn_ref[...]

        def fin(rows):
            zn_ref[rows, :] = _rms_rows(o_ref[rows, :], gn).astype(zn_ref.dtype)

        _row_loop(tm, fin)

        @pl.when(i == n_i - 1)
        def _():
            for c in range(X_COPY_CHUNKS):
                o_copy(c, i).wait()


def _ffn(x, g, wg, wu, wd, g_next, side=()):
    m, d = x.shape
    tm = min(FFN_TM, m)
    tile_major = wg.ndim == 3
    tf = wg.shape[2] if tile_major else FFN_TF
    n_f = wg.shape[0] if tile_major else wg.shape[1] // tf
    n_i = m // tm
    if tile_major:
        w_spec = pl.BlockSpec((None, d, tf), lambda i, f: (f, 0, 0))
    else:
        w_spec = pl.BlockSpec((d, tf), lambda i, f: (0, f))
    assert tm % (X_COPY_CHUNKS * NORM_ROWS) == 0
    plans = [_cast_plan(w.shape, bc, n_i * n_f) for w, bc in side]
    assert all(p is not None for p in plans)

    def side_specs(plan):
        br, bc, nr, nc = plan

        def block(i, f):
            b = jnp.minimum(i * n_f + f, nr * nc - 1)
            return b // nc, b % nc

        def src_index(i, f):
            return block(i, f)

        def dst_index(i, f):
            rb, cb = block(i, f)
            return cb, rb, 0

        return pl.BlockSpec((br, bc), src_index), pl.BlockSpec((None, br, bc), dst_index)

    specs = [side_specs(p) for p in plans]
    est = (tm * d * 4 + 2 * tm * d * 2 + 2 * 3 * d * tf * 2
           + sum(2 * p[0] * p[1] * 6 for p in plans))
    outs = pl.pallas_call(
        functools.partial(_ffn_body, n_i=n_i, n_f=n_f, tm=tm, n_side=len(side)),
        out_shape=(jax.ShapeDtypeStruct((m, d), F32), jax.ShapeDtypeStruct((m, d), BF16))
        + tuple(jax.ShapeDtypeStruct((w.shape[1] // bc, w.shape[0], bc), BF16)
                for w, bc in side),
        grid=(n_i, n_f),
        in_specs=[
            pl.BlockSpec(memory_space=pl.ANY),
            pl.BlockSpec((1, d), lambda i, f: (0, 0)),
            pl.BlockSpec((1, d), lambda i, f: (0, 0)),
            w_spec,
            w_spec,
            pl.BlockSpec((tf, d), lambda i, f: (f, 0)),
        ] + [src for src, _ in specs],
        out_specs=(
            pl.BlockSpec(memory_space=pl.ANY),
            pl.BlockSpec((tm, d), lambda i, f: (i, 0)),
        ) + tuple(dst for _, dst in specs),
        scratch_shapes=[pltpu.VMEM((tm, d), F32),
                        pltpu.SemaphoreType.DMA((X_COPY_CHUNKS,)),
                        pltpu.SemaphoreType.DMA((X_COPY_CHUNKS,))],
        compiler_params=_params(("arbitrary", "arbitrary"), est),
        name="ffn",
    )(x, g, g_next, wg, wu, wd, *[w for w, _ in side])
    return outs[0], outs[1], outs[2:]


def _matmul_body(*refs, has_h, has_p):
    z_ref, w_ref = refs[:2]
    o_ref = refs[-1]
    acc = jnp.dot(z_ref[...], w_ref[...], preferred_element_type=F32)
    if has_p:
        h_ref, p_ref, wp_ref = refs[2:5]
        proj = jnp.dot(p_ref[...].astype(BF16), wp_ref[...], preferred_element_type=F32)
        o_ref[...] = h_ref[...] + jax.nn.sigmoid(acc) * proj
    elif has_h:
        o_ref[...] = refs[2][...] + acc
    else:
        o_ref[...] = acc


def _matmul(z, w, tm, h=None, p=None, wp=None, name="matmul"):
    m, k = z.shape
    n_j, _, tn = w.shape
    n = n_j * tn
    tm = min(tm, m)
    operands = [z, w]
    in_specs = [pl.BlockSpec((tm, k), lambda i, j: (i, 0)),
                pl.BlockSpec((None, k, tn), lambda i, j: (j, 0, 0))]
    est = 2 * tm * k * 2 + 2 * k * tn * 2 + 2 * tm * tn * 4
    if h is not None:
        operands.append(h)
        in_specs.append(pl.BlockSpec((tm, tn), lambda i, j: (i, j)))
        est += 2 * tm * tn * 4
    if p is not None:
        pd = p.shape[1]
        operands += [p, wp]
        in_specs += [pl.BlockSpec((tm, pd), lambda i, j: (i, 0)),
                     pl.BlockSpec((None, pd, tn), lambda i, j: (j, 0, 0))]
        est += 2 * tm * pd * 4 + 2 * pd * tn * 2
    return pl.pallas_call(
        functools.partial(_matmul_body, has_h=h is not None, has_p=p is not None),
        out_shape=jax.ShapeDtypeStruct((m, n), F32),
        grid=(m // tm, n_j),
        in_specs=in_specs,
        out_specs=pl.BlockSpec((tm, tn), lambda i, j: (i, j)),
        compiler_params=_params(("arbitrary", "arbitrary"), est),
        name=name,
    )(*operands)


def _stack_cols(u_ref, n_s, off, width, count):
    return jnp.concatenate(
        [u_ref[s, :, off + j * width:off + (j + 1) * width]
         for s in range(n_s) for j in range(count)], axis=0)


def _bdot_nt(a, b):
    return lax.dot_general(a, b, (((2,), (2,)), ((0,), (0,))), preferred_element_type=F32)


def _bdot_nn(a, b):
    return lax.dot_general(a, b, (((2,), (1,)), ((0,), (0,))), preferred_element_type=F32)


def _new_kv(u_ref, gk, n_s, kwin_ref, vwin_ref):
    nb = n_s * N_KV_HEADS
    knew = _rms_rows(_stack_cols(u_ref, n_s, K_OFF, HEAD_DIM, N_KV_HEADS), gk)
    vnew = _stack_cols(u_ref, n_s, V_OFF, HEAD_DIM, N_KV_HEADS)
    kwin_ref[:, WINDOW:WINDOW + CHUNK, :] = knew.reshape(nb, CHUNK, HEAD_DIM)
    vwin_ref[:, WINDOW:WINDOW + CHUNK, :] = vnew.reshape(nb, CHUNK, HEAD_DIM)


def _attention_chunk(u_ref, kwin_ref, vwin_ref, sink_ref, gq, c_first, mix_ref, n_s):
    nb = n_s * N_KV_HEADS
    nq = GROUP * CHUNK
    q = _rms_rows(_stack_cols(u_ref, n_s, Q_OFF, HEAD_DIM, N_HEADS), gq)
    q = q.astype(BF16).reshape(nb, nq, HEAD_DIM)
    s = _bdot_nt(kwin_ref[...].astype(BF16), q) * (HEAD_DIM ** -0.5 * LOG2_E)
    if c_first is not None:
        key_chunk = lax.broadcasted_iota(jnp.int32, (1, WINDOW + CHUNK, nq), 1) // CHUNK
        s = jnp.where((c_first + key_chunk) >= (WINDOW // CHUNK), s, -jnp.inf)
    sink = sink_ref[...] * LOG2_E
    m = jnp.maximum(jnp.max(s, axis=1, keepdims=True), sink)
    e = jnp.exp2(s - m)
    denom = jnp.sum(e, axis=1, keepdims=True) + jnp.exp2(sink - m)
    v_t = jnp.swapaxes(vwin_ref[...], 1, 2).astype(BF16)
    o_t = _bdot_nn(v_t, e.astype(BF16)) / denom
    o = jnp.swapaxes(o_t, 1, 2)
    for st in range(n_s):
        pieces = [o[st * N_KV_HEADS + k, g * CHUNK:(g + 1) * CHUNK, :]
                  for k in range(N_KV_HEADS) for g in range(GROUP)]
        mix_ref[st, :, 0:ATTN_WIDTH] = jnp.concatenate(pieces, axis=-1).astype(mix_ref.dtype)


def _retention_chunk(u_ref, rope_ref, lg_ref, gret_ref, pos0, state, mix_ref, n_s):
    nb = n_s * RET_HEADS
    row = lax.broadcasted_iota(jnp.int32, (CHUNK, 1), 0)
    pos = (pos0 + row).astype(F32)
    ang = pos * rope_ref[0:1, :]
    cosv = jnp.cos(ang)[None]
    sinv = (jnp.sin(ang) * rope_ref[1:2, :])[None]
    lg = lg_ref[...]
    ri = lax.broadcasted_iota(jnp.int32, (CHUNK, CHUNK), 0)
    ci = lax.broadcasted_iota(jnp.int32, (CHUNK, CHUNK), 1)
    dist = jnp.abs(ri - ci).astype(F32)[None]
    rowf = row.astype(F32)[None]

    def rope(x):
        rolled = pltpu.roll(x, RET_DK // 2, 1)
        return (x.reshape(nb, CHUNK, RET_DK) * cosv
                + rolled.reshape(nb, CHUNK, RET_DK) * sinv)

    q = rope(_stack_cols(u_ref, n_s, QR_OFF, RET_DK, RET_HEADS))
    kk = rope(_stack_cols(u_ref, n_s, KR_OFF, RET_DK, RET_HEADS)) * (RET_DK ** -0.5)
    v = _stack_cols(u_ref, n_s, VR_OFF, RET_DV, RET_HEADS).astype(BF16)
    v = v.reshape(nb, CHUNK, RET_DV)
    qb = q.astype(BF16)
    sc = _bdot_nt(qb, kk.astype(BF16)) * jnp.exp(lg * dist)
    out = _bdot_nn(sc.astype(BF16), v)
    out = out + _bdot_nn(qb, state.astype(BF16)) * jnp.exp(lg * (rowf + 1.0))
    kd = kk * jnp.exp(lg * ((CHUNK - 1.0) - rowf))
    upd = _bdot_nn(jnp.swapaxes(kd, 1, 2).astype(BF16), v)
    new_state = jnp.exp(lg * float(CHUNK)) * state + upd
    mu = jnp.mean(out, axis=-1, keepdims=True)
    cen = out - mu
    var = jnp.mean(cen * cen, axis=-1, keepdims=True)
    normed = cen * lax.rsqrt(var + EPS) * gret_ref[...]
    for st in range(n_s):
        gate = u_ref[st, :, GR_OFF:GR_OFF + RET_WIDTH]
        flat = jnp.concatenate([normed[st * RET_HEADS + h] for h in range(RET_HEADS)], axis=-1)
        ret = flat * (gate * jax.nn.sigmoid(gate))
        mix_ref[st, :, ATTN_WIDTH:ATTN_WIDTH + RET_WIDTH] = ret.astype(mix_ref.dtype)
    return new_state


def _window_to_rows(win_ref, st):
    return jnp.concatenate([win_ref[st * N_KV_HEADS + k, CHUNK:WINDOW + CHUNK, :]
                            for k in range(N_KV_HEADS)], axis=-1)


def _mixer_prompt_body(u_ref, sink_ref, gq_ref, gk_ref, gret_ref, lg_ref, rope_ref,
                       mix_ref, ko_ref, vo_ref, st_ref, kwin_ref, vwin_ref, *, n_chunks, n_s):
    c = pl.program_id(0)
    nbk = n_s * N_KV_HEADS

    @pl.when(c == 0)
    def _():
        kwin_ref[:, 0:WINDOW, :] = jnp.zeros((nbk, WINDOW, HEAD_DIM), F32)
        vwin_ref[:, 0:WINDOW, :] = jnp.zeros((nbk, WINDOW, HEAD_DIM), F32)
        st_ref[...] = jnp.zeros(st_ref.shape, F32)

    _new_kv(u_ref, gk_ref[...], n_s, kwin_ref, vwin_ref)
    _attention_chunk(u_ref, kwin_ref, vwin_ref, sink_ref, gq_ref[...], c, mix_ref, n_s)

    state = st_ref[...].reshape(n_s * RET_HEADS, RET_DK, RET_DV)
    new_state = _retention_chunk(u_ref, rope_ref, lg_ref, gret_ref, c * CHUNK, state, mix_ref, n_s)
    st_ref[...] = new_state.reshape(st_ref.shape)

    @pl.when(c == n_chunks - 1)
    def _():
        for st in range(n_s):
            ko_ref[st] = _window_to_rows(kwin_ref, st)
            vo_ref[st] = _window_to_rows(vwin_ref, st)

    kshift = kwin_ref[:, CHUNK:WINDOW + CHUNK, :]
    vshift = vwin_ref[:, CHUNK:WINDOW + CHUNK, :]
    kwin_ref[:, 0:WINDOW, :] = kshift
    vwin_ref[:, 0:WINDOW, :] = vshift


def _const_spec(shape):
    return pl.BlockSpec(shape, lambda *_: (0,) * len(shape))


def _mixer_prompt(u, sink_col, gq, gk, gret, lg, rope, batch, seq):
    n_chunks = seq // CHUNK
    n_s = batch
    est = (2 * n_s * CHUNK * IN_WIDTH * 4 + 2 * n_s * RET_HEADS * RET_DK * RET_DV * 4
           + 16 * 1024 * 1024)
    return pl.pallas_call(
        functools.partial(_mixer_prompt_body, n_chunks=n_chunks, n_s=n_s),
        out_shape=(
            jax.ShapeDtypeStruct((batch, seq, D_MODEL), BF16),
            jax.ShapeDtypeStruct((batch, WINDOW, KV_COLS), F32),
            jax.ShapeDtypeStruct((batch, WINDOW, KV_COLS), F32),
            jax.ShapeDtypeStruct((batch, RET_HEADS, RET_DK, RET_DV), F32),
        ),
        grid=(n_chunks,),
        in_specs=[
            pl.BlockSpec((n_s, CHUNK, IN_WIDTH), lambda c: (0, c, 0)),
            _const_spec(sink_col.shape),
            _const_spec(gq.shape),
            _const_spec(gk.shape),
            _const_spec(gret.shape),
            _const_spec(lg.shape),
            _const_spec(rope.shape),
        ],
        out_specs=(
            pl.BlockSpec((n_s, CHUNK, D_MODEL), lambda c: (0, c, 0)),
            _const_spec((batch, WINDOW, KV_COLS)),
            _const_spec((batch, WINDOW, KV_COLS)),
            _const_spec((batch, RET_HEADS, RET_DK, RET_DV)),
        ),
        scratch_shapes=[pltpu.VMEM((n_s * N_KV_HEADS, WINDOW + CHUNK, HEAD_DIM), F32),
                        pltpu.VMEM((n_s * N_KV_HEADS, WINDOW + CHUNK, HEAD_DIM), F32)],
        compiler_params=_params(("arbitrary",), est),
        name="mixer_prompt",
    )(u.reshape(batch, seq, IN_WIDTH), sink_col, gq, gk, gret, lg, rope)


def _mixer_sample_body(u_ref, ck_ref, cv_ref, sin_ref, sink_ref, gq_ref, gk_ref, gret_ref,
                       lg_ref, rope_ref, mix_ref, ko_ref, vo_ref, st_ref, kwin_ref, vwin_ref,
                       *, n_s):
    for st in range(n_s):
        for k in range(N_KV_HEADS):
            cols = slice(k * HEAD_DIM, (k + 1) * HEAD_DIM)
            kwin_ref[st * N_KV_HEADS + k, 0:WINDOW, :] = ck_ref[st, :, cols]
            vwin_ref[st * N_KV_HEADS + k, 0:WINDOW, :] = cv_ref[st, :, cols]
    _new_kv(u_ref, gk_ref[...], n_s, kwin_ref, vwin_ref)
    _attention_chunk(u_ref, kwin_ref, vwin_ref, sink_ref, gq_ref[...], None, mix_ref, n_s)
    state = sin_ref[...].reshape(n_s * RET_HEADS, RET_DK, RET_DV)
    new_state = _retention_chunk(u_ref, rope_ref, lg_ref, gret_ref, PAST_LEN, state, mix_ref, n_s)
    st_ref[...] = new_state.reshape(st_ref.shape)
    for st in range(n_s):
        ko_ref[st] = _window_to_rows(kwin_ref, st)
        vo_ref[st] = _window_to_rows(vwin_ref, st)


def _mixer_sample(u, cache_k, cache_v, state, sink_col, gq, gk, gret, lg, rope, n_s):
    nb = cache_k.shape[0]
    est = (2 * n_s * CHUNK * IN_WIDTH * 4 + 4 * n_s * RET_HEADS * RET_DK * RET_DV * 4
           + 16 * 1024 * 1024)
    stream3 = lambda i: (i, 0, 0)
    return pl.pallas_call(
        functools.partial(_mixer_sample_body, n_s=n_s),
        out_shape=(
            jax.ShapeDtypeStruct((nb, CHUNK, D_MODEL), BF16),
            jax.ShapeDtypeStruct((nb, WINDOW, KV_COLS), F32),
            jax.ShapeDtypeStruct((nb, WINDOW, KV_COLS), F32),
            jax.ShapeDtypeStruct((nb, RET_HEADS, RET_DK, RET_DV), F32),
        ),
        grid=(nb // n_s,),
        in_specs=[
            pl.BlockSpec((n_s, CHUNK, IN_WIDTH), stream3),
            pl.BlockSpec((n_s, WINDOW, KV_COLS), stream3),
            pl.BlockSpec((n_s, WINDOW, KV_COLS), stream3),
            pl.BlockSpec((n_s, RET_HEADS, RET_DK, RET_DV), lambda i: (i, 0, 0, 0)),
            _const_spec(sink_col.shape),
            _const_spec(gq.shape),
            _const_spec(gk.shape),
            _const_spec(gret.shape),
            _const_spec(lg.shape),
            _const_spec(rope.shape),
        ],
        out_specs=(
            pl.BlockSpec((n_s, CHUNK, D_MODEL), stream3),
            pl.BlockSpec((n_s, WINDOW, KV_COLS), stream3),
            pl.BlockSpec((n_s, WINDOW, KV_COLS), stream3),
            pl.BlockSpec((n_s, RET_HEADS, RET_DK, RET_DV), lambda i: (i, 0, 0, 0)),
        ),
        scratch_shapes=[pltpu.VMEM((n_s * N_KV_HEADS, WINDOW + CHUNK, HEAD_DIM), F32),
                        pltpu.VMEM((n_s * N_KV_HEADS, WINDOW + CHUNK, HEAD_DIM), F32)],
        compiler_params=_params(("arbitrary",), est),
        name="mixer_sample",
    )(u.reshape(nb, CHUNK, IN_WIDTH), cache_k, cache_v, state, sink_col, gq, gk, gret, lg, rope)


def _mixer_tables(sinks, gret, n_s):
    sink_col = jnp.repeat(sinks.astype(F32), CHUNK).reshape(N_KV_HEADS, 1, GROUP * CHUNK)
    sink_col = jnp.tile(sink_col, (n_s, 1, 1))
    gret3 = jnp.tile(gret.astype(F32).reshape(RET_HEADS, 1, RET_DV), (n_s, 1, 1))
    lg = jnp.tile(jnp.asarray(np.array(LOG_GAMMA, np.float32).reshape(RET_HEADS, 1, 1)),
                  (n_s, 1, 1))
    return sink_col, gret3, lg


def _rope_table():
    half = RET_DK // 2
    inv_freq = (np.float32(ROPE_BASE) ** (-np.arange(half, dtype=np.float32) / np.float32(half)))
    inv_freq = inv_freq.astype(np.float32)
    sign = np.concatenate([-np.ones(half, np.float32), np.ones(half, np.float32)])
    return jnp.asarray(np.stack([np.concatenate([inv_freq, inv_freq]), sign]))


def kernel(x_prompt, x_sample, cache_attn_k, cache_attn_v, state_ret, p_prompt, p_sample, g_ffn1, w_ffn1_gate, w_ffn1_up, w_ffn1_down, g_mix, w_in, g_q, g_k, attn_sinks, g_ret, w_out, g_ffn2, w_ffn2_gate, w_ffn2_up, w_ffn2_down, g_ple, w_ple_gate, w_ple_proj):
    depth = g_ffn1.shape[0]
    batch, seq, d = x_prompt.shape
    nb, dec_len, _ = x_sample.shape
    assert d == D_MODEL and dec_len == CHUNK and seq % CHUNK == 0
    assert w_in.shape[2] == IN_WIDTH and nb % SAMPLE_STREAMS == 0
    rope = _rope_table()

    hp = x_prompt.reshape(batch * seq, d)
    hs = x_sample.reshape(nb * dec_len, d)
    kp_l, vp_l, sp_l, ks_l, vs_l, ss_l = [], [], [], [], [], []
    for i in range(depth):
        wg1 = w_ffn1_gate[i].astype(BF16)
        wu1 = w_ffn1_up[i].astype(BF16)
        wd1 = w_ffn1_down[i].astype(BF16)
        gf1 = g_ffn1[i].reshape(1, d)
        gf2 = g_ffn2[i].reshape(1, d)
        gmx = g_mix[i].reshape(1, d)
        gpl = g_ple[i].reshape(1, d)
        gq = g_q[i].reshape(1, HEAD_DIM)
        gk = g_k[i].reshape(1, HEAD_DIM)
        sinks = attn_sinks[i]
        gret = g_ret[i]

        later_p = ((w_ffn2_gate[i], FFN_TF), (w_ffn2_up[i], FFN_TF), (w_ffn2_down[i], d))
        later_s = ((w_in[i], PROJ_TN), (w_out[i], MERGE_TN), (w_ple_gate[i], PLE_TN),
                   (w_ple_proj[i], PLE_TN))
        hp, zp, (wg2, wu2, wd2) = _ffn(hp, gf1, wg1, wu1, wd1, gmx, later_p)
        wd2 = wd2.reshape(wd2.shape[1], d)
        hs, zs, (win, wout, wpg, wpp) = _ffn(hs, gf1, wg1, wu1, wd1, gmx, later_s)

        up = _matmul(zp, win, PROJ_TM, name="proj")
        us = _matmul(zs, win, PROJ_TM, name="proj")
        sink_p, gret_p, lg_p = _mixer_tables(sinks, gret, batch)
        sink_s, gret_s, lg_s = _mixer_tables(sinks, gret, SAMPLE_STREAMS)
        mixp, kp, vp, sp = _mixer_prompt(up, sink_p, gq, gk, gret_p, lg_p, rope, batch, seq)
        mixs, ks, vs, ss = _mixer_sample(
            us, cache_attn_k[i].reshape(nb, WINDOW, KV_COLS),
            cache_attn_v[i].reshape(nb, WINDOW, KV_COLS), state_ret[i],
            sink_s, gq, gk, gret_s, lg_s, rope, SAMPLE_STREAMS)
        mixp = mixp.reshape(batch * seq, d)
        mixs = mixs.reshape(nb * dec_len, d)
        hp = _matmul(mixp, wout, MERGE_TM, h=hp, name="merge")
        hs = _matmul(mixs, wout, MERGE_TM, h=hs, name="merge")
        kp_l.append(kp.reshape(batch, WINDOW, N_KV_HEADS, HEAD_DIM))
        vp_l.append(vp.reshape(batch, WINDOW, N_KV_HEADS, HEAD_DIM))
        sp_l.append(sp)
        ks_l.append(ks.reshape(nb, WINDOW, N_KV_HEADS, HEAD_DIM))
        vs_l.append(vs.reshape(nb, WINDOW, N_KV_HEADS, HEAD_DIM))
        ss_l.append(ss)

        hp, zp, _ = _ffn(hp, gf2, wg2, wu2, wd2, gpl)
        hs, zs, _ = _ffn(hs, gf2, wg2, wu2, wd2, gpl)

        hp = _matmul(zp, wpg, PLE_TM, h=hp, p=p_prompt[i].reshape(batch * seq, -1),
                     wp=wpp, name="ple")
        hs = _matmul(zs, wpg, PLE_TM, h=hs, p=p_sample[i].reshape(nb * dec_len, -1),
                     wp=wpp, name="ple")

    return (hp.reshape(batch, seq, d), hs.reshape(nb, dec_len, d),
            jnp.stack(kp_l), jnp.stack(vp_l), jnp.stack(sp_l),
            jnp.stack(ks_l), jnp.stack(vs_l), jnp.stack(ss_l))
```

```python
import functools
import math

import numpy as np
import jax
import jax.numpy as jnp
from jax import lax
from jax.experimental import pallas as pl
from jax.experimental.pallas import tpu as pltpu

F32 = jnp.float32
BF16 = jnp.bfloat16

D_MODEL = 4096
CHUNK = 64
HEAD_DIM = 64
N_HEADS = 32
N_KV_HEADS = 4
GROUP = N_HEADS // N_KV_HEADS
WINDOW = 128
RET_HEADS = 8
RET_DK = 128
RET_DV = 256
ATTN_WIDTH = N_HEADS * HEAD_DIM
RET_WIDTH = RET_HEADS * RET_DV
PAST_LEN = 4096
ROPE_BASE = 10000.0
EPS = 1e-6

Q_OFF = 0
K_OFF = Q_OFF + N_HEADS * HEAD_DIM
V_OFF = K_OFF + N_KV_HEADS * HEAD_DIM
QR_OFF = V_OFF + N_KV_HEADS * HEAD_DIM
KR_OFF = QR_OFF + RET_HEADS * RET_DK
VR_OFF = KR_OFF + RET_HEADS * RET_DK
GR_OFF = VR_OFF + RET_WIDTH
IN_WIDTH = GR_OFF + RET_WIDTH
KV_COLS = N_KV_HEADS * HEAD_DIM

LOG2_E = math.log2(math.e)
LOG_GAMMA = tuple(math.log(1.0 - 2.0 ** (-5.0 - h)) for h in range(RET_HEADS))

VMEM_CAP_BYTES = 60000 * 1024

FFN_TM = 1024
FFN_TF = 256
PROJ_TM, PROJ_TN = 2048, 512
MERGE_TM, MERGE_TN = 1024, 1024
PLE_TM, PLE_TN = 1024, 512
NORM_ROWS = 16
SAMPLE_STREAMS = 2
ROW_LOOP_UNROLL = 8
X_COPY_CHUNKS = 4


def _vmem_limit(estimate_bytes):
    return int(min(VMEM_CAP_BYTES, max(32 * 1024 * 1024, estimate_bytes * 5 // 4)))


def _params(semantics, vmem_estimate):
    return pltpu.CompilerParams(dimension_semantics=semantics,
                                vmem_limit_bytes=_vmem_limit(vmem_estimate))


def _row_loop(n_rows, body):
    def step(r, carry):
        body(pl.ds(pl.multiple_of(r * NORM_ROWS, NORM_ROWS), NORM_ROWS))
        return carry

    lax.fori_loop(0, n_rows // NORM_ROWS, step, 0, unroll=ROW_LOOP_UNROLL)


def _rms_rows(x, g):
    ms = jnp.mean(x * x, axis=-1, keepdims=True)
    return x * lax.rsqrt(ms + EPS) * g


def _cast_plan(shape, bc, n_steps):
    r, c = shape
    nc = c // bc
    fits = [nr for nr in range(1, r + 1)
            if r % nr == 0 and (r // nr) % 16 == 0 and nr * nc <= n_steps]
    if not fits:
        return None
    nr = max(fits)
    return r // nr, bc, nr, nc


def _ffn_body(*refs, n_i, n_f, tm, n_side):
    x_hbm, g_ref, gn_ref, wg_ref, wu_ref, wd_ref = refs[:6]
    side_in = refs[6:6 + n_side]
    o_hbm, zn_ref = refs[6 + n_side:8 + n_side]
    side_out = refs[8 + n_side:8 + 2 * n_side]
    o_ref, x_sems, o_sems = refs[8 + 2 * n_side:]
    i = pl.program_id(0)
    f = pl.program_id(1)
    rc = tm // X_COPY_CHUNKS

    def x_copy(c):
        return pltpu.make_async_copy(
            x_hbm.at[pl.ds(pl.multiple_of(i * tm, tm) + c * rc, rc), :],
            o_ref.at[pl.ds(c * rc, rc), :], x_sems.at[c])

    def o_copy(c, tile):
        return pltpu.make_async_copy(
            o_ref.at[pl.ds(c * rc, rc), :],
            o_hbm.at[pl.ds(pl.multiple_of(tile * tm, tm) + c * rc, rc), :], o_sems.at[c])

    @pl.when(f == 0)
    def _():
        for c in range(X_COPY_CHUNKS):
            @pl.when(i > 0)
            def _(c=c):
                o_copy(c, i - 1).wait()

            x_copy(c).start(priority=c % 2)
        g = g_ref[...]
        for c in range(X_COPY_CHUNKS):
            x_copy(c).wait()

            def init(rows, base=c * rc):
                rows = pl.ds(rows.start + base, NORM_ROWS)
                zn_ref[rows, :] = _rms_rows(o_ref[rows, :], g).astype(zn_ref.dtype)

            _row_loop(rc, init)

    for src, dst in zip(side_in, side_out):
        dst[...] = src[...].astype(dst.dtype)

    z = zn_ref[...]
    gate = jnp.dot(z, wg_ref[...], preferred_element_type=F32)
    up = jnp.dot(z, wu_ref[...], preferred_element_type=F32)
    a = (0.5 * gate * jax.nn.sigmoid(gate) * up).astype(BF16)
    o_ref[...] += jnp.dot(a, wd_ref[...], preferred_element_type=F32)

    @pl.when(f == n_f - 1)
    def _():
        for c in range(X_COPY_CHUNKS):
            o_copy(c, i).start(priority=c % 2)
        gn = gn_ref[...]

        def fin(rows):
            zn_ref[rows, :] = _rms_rows(o_ref[rows, :], gn).astype(zn_ref.dtype)

        _row_loop(tm, fin)

        @pl.when(i == n_i - 1)
        def _():
            for c in range(X_COPY_CHUNKS):
                o_copy(c, i).wait()


def _ffn(x, g, wg, wu, wd, g_next, side=()):
    m, d = x.shape
    tm = min(FFN_TM, m)
    tile_major = wg.ndim == 3
    tf = wg.shape[2] if tile_major else FFN_TF
    n_f = wg.shape[0] if tile_major else wg.shape[1] // tf
    n_i = m // tm
    if tile_major:
        w_spec = pl.BlockSpec((None, d, tf), lambda i, f: (f, 0, 0))
    else:
        w_spec = pl.BlockSpec((d, tf), lambda i, f: (0, f))
    assert tm % (X_COPY_CHUNKS * NORM_ROWS) == 0
    plans = [_cast_plan(w.shape, bc, n_i * n_f) for w, bc in side]
    assert all(p is not None for p in plans)

    def side_specs(plan):
        br, bc, nr, nc = plan

        def block(i, f):
            b = jnp.minimum(i * n_f + f, nr * nc - 1)
            return b // nc, b % nc

        def src_index(i, f):
            return block(i, f)

        def dst_index(i, f):
            rb, cb = block(i, f)
            return cb, rb, 0

        return pl.BlockSpec((br, bc), src_index), pl.BlockSpec((None, br, bc), dst_index)

    specs = [side_specs(p) for p in plans]
    est = (tm * d * 4 + 2 * tm * d * 2 + 2 * 3 * d * tf * 2
           + sum(2 * p[0] * p[1] * 6 for p in plans))
    outs = pl.pallas_call(
        functools.partial(_ffn_body, n_i=n_i, n_f=n_f, tm=tm, n_side=len(side)),
        out_shape=(jax.ShapeDtypeStruct((m, d), F32), jax.ShapeDtypeStruct((m, d), BF16))
        + tuple(jax.ShapeDtypeStruct((w.shape[1] // bc, w.shape[0], bc), BF16)
                for w, bc in side),
        grid=(n_i, n_f),
        in_specs=[
            pl.BlockSpec(memory_space=pl.ANY),
            pl.BlockSpec((1, d), lambda i, f: (0, 0)),
            pl.BlockSpec((1, d), lambda i, f: (0, 0)),
            w_spec,
            w_spec,
            pl.BlockSpec((tf, d), lambda i, f: (f, 0)),
        ] + [src for src, _ in specs],
        out_specs=(
            pl.BlockSpec(memory_space=pl.ANY),
            pl.BlockSpec((tm, d), lambda i, f: (i, 0)),
        ) + tuple(dst for _, dst in specs),
        scratch_shapes=[pltpu.VMEM((tm, d), F32),
                        pltpu.SemaphoreType.DMA((X_COPY_CHUNKS,)),
                        pltpu.SemaphoreType.DMA((X_COPY_CHUNKS,))],
        compiler_params=_params(("arbitrary", "arbitrary"), est),
        name="ffn",
    )(x, g, g_next, wg, wu, wd, *[w for w, _ in side])
    return outs[0], outs[1], outs[2:]


def _matmul_body(*refs, has_h, has_p):
    z_ref, w_ref = refs[:2]
    o_ref = refs[-1]
    acc = jnp.dot(z_ref[...], w_ref[...], preferred_element_type=F32)
    if has_p:
        h_ref, p_ref, wp_ref = refs[2:5]
        proj = jnp.dot(p_ref[...].astype(BF16), wp_ref[...], preferred_element_type=F32)
        o_ref[...] = h_ref[...] + jax.nn.sigmoid(acc) * proj
    elif has_h:
        o_ref[...] = refs[2][...] + acc
    else:
        o_ref[...] = acc


def _matmul(z, w, tm, h=None, p=None, wp=None, name="matmul"):
    m, k = z.shape
    n_j, _, tn = w.shape
    n = n_j * tn
    tm = min(tm, m)
    operands = [z, w]
    in_specs = [pl.BlockSpec((tm, k), lambda i, j: (i, 0)),
                pl.BlockSpec((None, k, tn), lambda i, j: (j, 0, 0))]
    est = 2 * tm * k * 2 + 2 * k * tn * 2 + 2 * tm * tn * 4
    if h is not None:
        operands.append(h)
        in_specs.append(pl.BlockSpec((tm, tn), lambda i, j: (i, j)))
        est += 2 * tm * tn * 4
    if p is not None:
        pd = p.shape[1]
        operands += [p, wp]
        in_specs += [pl.BlockSpec((tm, pd), lambda i, j: (i, 0)),
                     pl.BlockSpec((None, pd, tn), lambda i, j: (j, 0, 0))]
        est += 2 * tm * pd * 4 + 2 * pd * tn * 2
    return pl.pallas_call(
        functools.partial(_matmul_body, has_h=h is not None, has_p=p is not None),
        out_shape=jax.ShapeDtypeStruct((m, n), F32),
        grid=(m // tm, n_j),
        in_specs=in_specs,
        out_specs=pl.BlockSpec((tm, tn), lambda i, j: (i, j)),
        compiler_params=_params(("arbitrary", "arbitrary"), est),
        name=name,
    )(*operands)


def _stack_cols(u_ref, n_s, off, width, count):
    return jnp.concatenate(
        [u_ref[s, :, off + j * width:off + (j + 1) * width]
         for s in range(n_s) for j in range(count)], axis=0)


def _bdot_nt(a, b):
    return lax.dot_general(a, b, (((2,), (2,)), ((0,), (0,))), preferred_element_type=F32)


def _bdot_nn(a, b):
    return lax.dot_general(a, b, (((2,), (1,)), ((0,), (0,))), preferred_element_type=F32)


def _new_kv(u_ref, gk, n_s, kwin_ref, vwin_ref):
    nb = n_s * N_KV_HEADS
    knew = _rms_rows(_stack_cols(u_ref, n_s, K_OFF, HEAD_DIM, N_KV_HEADS), gk)
    vnew = _stack_cols(u_ref, n_s, V_OFF, HEAD_DIM, N_KV_HEADS)
    kwin_ref[:, WINDOW:WINDOW + CHUNK, :] = knew.reshape(nb, CHUNK, HEAD_DIM)
    vwin_ref[:, WINDOW:WINDOW + CHUNK, :] = vnew.reshape(nb, CHUNK, HEAD_DIM)


def _attention_chunk(u_ref, kwin_ref, vwin_ref, sink_ref, gq, c_first, mix_ref, n_s):
    nb = n_s * N_KV_HEADS
    nq = GROUP * CHUNK
    q = _rms_rows(_stack_cols(u_ref, n_s, Q_OFF, HEAD_DIM, N_HEADS), gq)
    q = q.astype(BF16).reshape(nb, nq, HEAD_DIM)
    s = _bdot_nt(kwin_ref[...].astype(BF16), q) * (HEAD_DIM ** -0.5 * LOG2_E)
    if c_first is not None:
        key_chunk = lax.broadcasted_iota(jnp.int32, (1, WINDOW + CHUNK, nq), 1) // CHUNK
        s = jnp.where((c_first + key_chunk) >= (WINDOW // CHUNK), s, -jnp.inf)
    sink = sink_ref[...] * LOG2_E
    m = jnp.maximum(jnp.max(s, axis=1, keepdims=True), sink)
    e = jnp.exp2(s - m)
    denom = jnp.sum(e, axis=1, keepdims=True) + jnp.exp2(sink - m)
    v_t = jnp.swapaxes(vwin_ref[...], 1, 2).astype(BF16)
    o_t = _bdot_nn(v_t, e.astype(BF16)) / denom
    o = jnp.swapaxes(o_t, 1, 2)
    for st in range(n_s):
        pieces = [o[st * N_KV_HEADS + k, g * CHUNK:(g + 1) * CHUNK, :]
                  for k in range(N_KV_HEADS) for g in range(GROUP)]
        mix_ref[st, :, 0:ATTN_WIDTH] = jnp.concatenate(pieces, axis=-1).astype(mix_ref.dtype)


def _retention_chunk(u_ref, rope_ref, lg_ref, gret_ref, pos0, state, mix_ref, n_s):
    nb = n_s * RET_HEADS
    row = lax.broadcasted_iota(jnp.int32, (CHUNK, 1), 0)
    pos = (pos0 + row).astype(F32)
    ang = pos * rope_ref[0:1, :]
    cosv = jnp.cos(ang)[None]
    sinv = (jnp.sin(ang) * rope_ref[1:2, :])[None]
    lg = lg_ref[...]
    ri = lax.broadcasted_iota(jnp.int32, (CHUNK, CHUNK), 0)
    ci = lax.broadcasted_iota(jnp.int32, (CHUNK, CHUNK), 1)
    dist = jnp.abs(ri - ci).astype(F32)[None]
    rowf = row.astype(F32)[None]

    def rope(x):
        rolled = pltpu.roll(x, RET_DK // 2, 1)
        return (x.reshape(nb, CHUNK, RET_DK) * cosv
                + rolled.reshape(nb, CHUNK, RET_DK) * sinv)

    q = rope(_stack_cols(u_ref, n_s, QR_OFF, RET_DK, RET_HEADS))
    kk = rope(_stack_cols(u_ref, n_s, KR_OFF, RET_DK, RET_HEADS)) * (RET_DK ** -0.5)
    v = _stack_cols(u_ref, n_s, VR_OFF, RET_DV, RET_HEADS).astype(BF16)
    v = v.reshape(nb, CHUNK, RET_DV)
    qb = q.astype(BF16)
    sc = _bdot_nt(qb, kk.astype(BF16)) * jnp.exp(lg * dist)
    out = _bdot_nn(sc.astype(BF16), v)
    out = out + _bdot_nn(qb, state.astype(BF16)) * jnp.exp(lg * (rowf + 1.0))
    kd = kk * jnp.exp(lg * ((CHUNK - 1.0) - rowf))
    upd = _bdot_nn(jnp.swapaxes(kd, 1, 2).astype(BF16), v)
    new_state = jnp.exp(lg * float(CHUNK)) * state + upd
    mu = jnp.mean(out, axis=-1, keepdims=True)
    cen = out - mu
    var = jnp.mean(cen * cen, axis=-1, keepdims=True)
    normed = cen * lax.rsqrt(var + EPS) * gret_ref[...]
    for st in range(n_s):
        gate = u_ref[st, :, GR_OFF:GR_OFF + RET_WIDTH]
        flat = jnp.concatenate([normed[st * RET_HEADS + h] for h in range(RET_HEADS)], axis=-1)
        ret = flat * (gate * jax.nn.sigmoid(gate))
        mix_ref[st, :, ATTN_WIDTH:ATTN_WIDTH + RET_WIDTH] = ret.astype(mix_ref.dtype)
    return new_state


def _window_to_rows(win_ref, st):
    return jnp.concatenate([win_ref[st * N_KV_HEADS + k, CHUNK:WINDOW + CHUNK, :]
                            for k in range(N_KV_HEADS)], axis=-1)


def _mixer_prompt_body(u_ref, sink_ref, gq_ref, gk_ref, gret_ref, lg_ref, rope_ref,
                       mix_ref, ko_ref, vo_ref, st_ref, kwin_ref, vwin_ref, *, n_chunks, n_s):
    c = pl.program_id(0)
    nbk = n_s * N_KV_HEADS

    @pl.when(c == 0)
    def _():
        kwin_ref[:, 0:WINDOW, :] = jnp.zeros((nbk, WINDOW, HEAD_DIM), F32)
        vwin_ref[:, 0:WINDOW, :] = jnp.zeros((nbk, WINDOW, HEAD_DIM), F32)
        st_ref[...] = jnp.zeros(st_ref.shape, F32)

    _new_kv(u_ref, gk_ref[...], n_s, kwin_ref, vwin_ref)
    _attention_chunk(u_ref, kwin_ref, vwin_ref, sink_ref, gq_ref[...], c, mix_ref, n_s)

    state = st_ref[...].reshape(n_s * RET_HEADS, RET_DK, RET_DV)
    new_state = _retention_chunk(u_ref, rope_ref, lg_ref, gret_ref, c * CHUNK, state, mix_ref, n_s)
    st_ref[...] = new_state.reshape(st_ref.shape)

    @pl.when(c == n_chunks - 1)
    def _():
        for st in range(n_s):
            ko_ref[st] = _window_to_rows(kwin_ref, st)
            vo_ref[st] = _window_to_rows(vwin_ref, st)

    kshift = kwin_ref[:, CHUNK:WINDOW + CHUNK, :]
    vshift = vwin_ref[:, CHUNK:WINDOW + CHUNK, :]
    kwin_ref[:, 0:WINDOW, :] = kshift
    vwin_ref[:, 0:WINDOW, :] = vshift


def _const_spec(shape):
    return pl.BlockSpec(shape, lambda *_: (0,) * len(shape))


def _mixer_prompt(u, sink_col, gq, gk, gret, lg, rope, batch, seq):
    n_chunks = seq // CHUNK
    n_s = batch
    est = (2 * n_s * CHUNK * IN_WIDTH * 4 + 2 * n_s * RET_HEADS * RET_DK * RET_DV * 4
           + 16 * 1024 * 1024)
    return pl.pallas_call(
        functools.partial(_mixer_prompt_body, n_chunks=n_chunks, n_s=n_s),
        out_shape=(
            jax.ShapeDtypeStruct((batch, seq, D_MODEL), BF16),
            jax.ShapeDtypeStruct((batch, WINDOW, KV_COLS), F32),
            jax.ShapeDtypeStruct((batch, WINDOW, KV_COLS), F32),
            jax.ShapeDtypeStruct((batch, RET_HEADS, RET_DK, RET_DV), F32),
        ),
        grid=(n_chunks,),
        in_specs=[
            pl.BlockSpec((n_s, CHUNK, IN_WIDTH), lambda c: (0, c, 0)),
            _const_spec(sink_col.shape),
            _const_spec(gq.shape),
            _const_spec(gk.shape),
            _const_spec(gret.shape),
            _const_spec(lg.shape),
            _const_spec(rope.shape),
        ],
        out_specs=(
            pl.BlockSpec((n_s, CHUNK, D_MODEL), lambda c: (0, c, 0)),
            _const_spec((batch, WINDOW, KV_COLS)),
            _const_spec((batch, WINDOW, KV_COLS)),
            _const_spec((batch, RET_HEADS, RET_DK, RET_DV)),
        ),
        scratch_shapes=[pltpu.VMEM((n_s * N_KV_HEADS, WINDOW + CHUNK, HEAD_DIM), F32),
                        pltpu.VMEM((n_s * N_KV_HEADS, WINDOW + CHUNK, HEAD_DIM), F32)],
        compiler_params=_params(("arbitrary",), est),
        name="mixer_prompt",
    )(u.reshape(batch, seq, IN_WIDTH), sink_col, gq, gk, gret, lg, rope)


def _mixer_sample_body(u_ref, ck_ref, cv_ref, sin_ref, sink_ref, gq_ref, gk_ref, gret_ref,
                       lg_ref, rope_ref, mix_ref, ko_ref, vo_ref, st_ref, kwin_ref, vwin_ref,
                       *, n_s):
    for st in range(n_s):
        for k in range(N_KV_HEADS):
            cols = slice(k * HEAD_DIM, (k + 1) * HEAD_DIM)
            kwin_ref[st * N_KV_HEADS + k, 0:WINDOW, :] = ck_ref[st, :, cols]
            vwin_ref[st * N_KV_HEADS + k, 0:WINDOW, :] = cv_ref[st, :, cols]
    _new_kv(u_ref, gk_ref[...], n_s, kwin_ref, vwin_ref)
    _attention_chunk(u_ref, kwin_ref, vwin_ref, sink_ref, gq_ref[...], None, mix_ref, n_s)
    state = sin_ref[...].reshape(n_s * RET_HEADS, RET_DK, RET_DV)
    new_state = _retention_chunk(u_ref, rope_ref, lg_ref, gret_ref, PAST_LEN, state, mix_ref, n_s)
    st_ref[...] = new_state.reshape(st_ref.shape)
    for st in range(n_s):
        ko_ref[st] = _window_to_rows(kwin_ref, st)
        vo_ref[st] = _window_to_rows(vwin_ref, st)


def _mixer_sample(u, cache_k, cache_v, state, sink_col, gq, gk, gret, lg, rope, n_s):
    nb = cache_k.shape[0]
    est = (2 * n_s * CHUNK * IN_WIDTH * 4 + 4 * n_s * RET_HEADS * RET_DK * RET_DV * 4
           + 16 * 1024 * 1024)
    stream3 = lambda i: (i, 0, 0)
    return pl.pallas_call(
        functools.partial(_mixer_sample_body, n_s=n_s),
        out_shape=(
            jax.ShapeDtypeStruct((nb, CHUNK, D_MODEL), BF16),
            jax.ShapeDtypeStruct((nb, WINDOW, KV_COLS), F32),
            jax.ShapeDtypeStruct((nb, WINDOW, KV_COLS), F32),
            jax.ShapeDtypeStruct((nb, RET_HEADS, RET_DK, RET_DV), F32),
        ),
        grid=(nb // n_s,),
        in_specs=[
            pl.BlockSpec((n_s, CHUNK, IN_WIDTH), stream3),
            pl.BlockSpec((n_s, WINDOW, KV_COLS), stream3),
            pl.BlockSpec((n_s, WINDOW, KV_COLS), stream3),
            pl.BlockSpec((n_s, RET_HEADS, RET_DK, RET_DV), lambda i: (i, 0, 0, 0)),
            _const_spec(sink_col.shape),
            _const_spec(gq.shape),
            _const_spec(gk.shape),
            _const_spec(gret.shape),
            _const_spec(lg.shape),
            _const_spec(rope.shape),
        ],
        out_specs=(
            pl.BlockSpec((n_s, CHUNK, D_MODEL), stream3),
            pl.BlockSpec((n_s, WINDOW, KV_COLS), stream3),
            pl.BlockSpec((n_s, WINDOW, KV_COLS), stream3),
            pl.BlockSpec((n_s, RET_HEADS, RET_DK, RET_DV), lambda i: (i, 0, 0, 0)),
        ),
        scratch_shapes=[pltpu.VMEM((n_s * N_KV_HEADS, WINDOW + CHUNK, HEAD_DIM), F32),
                        pltpu.VMEM((n_s * N_KV_HEADS, WINDOW + CHUNK, HEAD_DIM), F32)],
        compiler_params=_params(("arbitrary",), est),
        name="mixer_sample",
    )(u.reshape(nb, CHUNK, IN_WIDTH), cache_k, cache_v, state, sink_col, gq, gk, gret, lg, rope)


def _mixer_tables(sinks, gret, n_s):
    sink_col = jnp.repeat(sinks.astype(F32), CHUNK).reshape(N_KV_HEADS, 1, GROUP * CHUNK)
    sink_col = jnp.tile(sink_col, (n_s, 1, 1))
    gret3 = jnp.tile(gret.astype(F32).reshape(RET_HEADS, 1, RET_DV), (n_s, 1, 1))
    lg = jnp.tile(jnp.asarray(np.array(LOG_GAMMA, np.float32).reshape(RET_HEADS, 1, 1)),
                  (n_s, 1, 1))
    return sink_col, gret3, lg


def _rope_table():
    half = RET_DK // 2
    inv_freq = (np.float32(ROPE_BASE) ** (-np.arange(half, dtype=np.float32) / np.float32(half)))
    inv_freq = inv_freq.astype(np.float32)
    sign = np.concatenate([-np.ones(half, np.float32), np.ones(half, np.float32)])
    return jnp.asarray(np.stack([np.concatenate([inv_freq, inv_freq]), sign]))


def kernel(x_prompt, x_sample, cache_attn_k, cache_attn_v, state_ret, p_prompt, p_sample, g_ffn1, w_ffn1_gate, w_ffn1_up, w_ffn1_down, g_mix, w_in, g_q, g_k, attn_sinks, g_ret, w_out, g_ffn2, w_ffn2_gate, w_ffn2_up, w_ffn2_down, g_ple, w_ple_gate, w_ple_proj):
    depth = g_ffn1.shape[0]
    batch, seq, d = x_prompt.shape
    nb, dec_len, _ = x_sample.shape
    assert d == D_MODEL and dec_len == CHUNK and seq % CHUNK == 0
    assert w_in.shape[2] == IN_WIDTH and nb % SAMPLE_STREAMS == 0
    rope = _rope_table()

    hp = x_prompt.reshape(batch * seq, d)
    hs = x_sample.reshape(nb * dec_len, d)
    kp_l, vp_l, sp_l, ks_l, vs_l, ss_l = [], [], [], [], [], []
    for i in range(depth):
        wg1 = w_ffn1_gate[i].astype(BF16)
        wu1 = w_ffn1_up[i].astype(BF16)
        wd1 = w_ffn1_down[i].astype(BF16)
        gf1 = g_ffn1[i].reshape(1, d)
        gf2 = g_ffn2[i].reshape(1, d)
        gmx = g_mix[i].reshape(1, d)
        gpl = g_ple[i].reshape(1, d)
        gq = g_q[i].reshape(1, HEAD_DIM)
        gk = g_k[i].reshape(1, HEAD_DIM)
        sinks = attn_sinks[i]
        gret = g_ret[i]

        later_p = ((w_ffn2_gate[i], FFN_TF), (w_ffn2_up[i], FFN_TF), (w_ffn2_down[i], d))
        later_s = ((w_in[i], PROJ_TN), (w_out[i], MERGE_TN), (w_ple_gate[i], PLE_TN),
                   (w_ple_proj[i], PLE_TN))
        hp, zp, (wg2, wu2, wd2) = _ffn(hp, gf1, wg1, wu1, wd1, gmx, later_p)
        wd2 = wd2.reshape(wd2.shape[1], d)
        hs, zs, (win, wout, wpg, wpp) = _ffn(hs, gf1, wg1, wu1, wd1, gmx, later_s)

        up = _matmul(zp, win, PROJ_TM, name="proj")
        us = _matmul(zs, win, PROJ_TM, name="proj")
        sink_p, gret_p, lg_p = _mixer_tables(sinks, gret, batch)
        sink_s, gret_s, lg_s = _mixer_tables(sinks, gret, SAMPLE_STREAMS)
        mixp, kp, vp, sp = _mixer_prompt(up, sink_p, gq, gk, gret_p, lg_p, rope, batch, seq)
        mixs, ks, vs, ss = _mixer_sample(
            us, cache_attn_k[i].reshape(nb, WINDOW, KV_COLS),
            cache_attn_v[i].reshape(nb, WINDOW, KV_COLS), state_ret[i],
            sink_s, gq, gk, gret_s, lg_s, rope, SAMPLE_STREAMS)
        mixp = mixp.reshape(batch * seq, d)
        mixs = mixs.reshape(nb * dec_len, d)
        hp = _matmul(mixp, wout, MERGE_TM, h=hp, name="merge")
        hs = _matmul(mixs, wout, MERGE_TM, h=hs, name="merge")
        kp_l.append(kp.reshape(batch, WINDOW, N_KV_HEADS, HEAD_DIM))
        vp_l.append(vp.reshape(batch, WINDOW, N_KV_HEADS, HEAD_DIM))
        sp_l.append(sp)
        ks_l.append(ks.reshape(nb, WINDOW, N_KV_HEADS, HEAD_DIM))
        vs_l.append(vs.reshape(nb, WINDOW, N_KV_HEADS, HEAD_DIM))
        ss_l.append(ss)

        hp, zp, _ = _ffn(hp, gf2, wg2, wu2, wd2, gpl)
        hs, zs, _ = _ffn(hs, gf2, wg2, wu2, wd2, gpl)

        hp = _matmul(zp, wpg, PLE_TM, h=hp, p=p_prompt[i].reshape(batch * seq, -1),
                     wp=wpp, name="ple")
        hs = _matmul(zs, wpg, PLE_TM, h=hs, p=p_sample[i].reshape(nb * dec_len, -1),
                     wp=wpp, name="ple")

    return (hp.reshape(batch, seq, d), hs.reshape(nb, dec_len, d),
            jnp.stack(kp_l), jnp.stack(vp_l), jnp.stack(sp_l),
            jnp.stack(ks_l), jnp.stack(vs_l), jnp.stack(ss_l))
```
